```python
import math
import jax, jax.numpy as jnp
from jax import lax
import numpy as np

D_MODEL = 1024
BATCH = 4
SEQ = 8192
DEPTH = 1

D_MIX = D_MODEL
D_ATTN = D_MIX // 2
D_SSM = D_MIX - D_ATTN
ATTN_HEADS = 4
ATTN_VDIM = D_ATTN // ATTN_HEADS
ATTN_QKDIM = ATTN_VDIM // 2
ROPE_DIM = ATTN_QKDIM // 4
ROPE_THETA = 500000.0
Q_BLOCK = 128
SSM_GROUP = 16
SSM_GROUPS = D_SSM // SSM_GROUP
SSM_STATE = 64
DT_MIN = 1e-3
DT_MAX = 1e-1
N_EXPERTS = 32
TOP_K = 4
D_FF = D_MODEL
SWIGLU_LIMIT = 7.0
SWIGLU_ALPHA = 1.702
MOE_BLOCK = 128
RMS_EPS = 1e-6

kernel_name = "hybrid_diffattn_s5_moe_layer"


def rmsnorm(x, g):
    xf = x.astype(jnp.float32)
    y = xf * lax.rsqrt(jnp.mean(xf * xf, axis=-1, keepdims=True) + RMS_EPS)
    return (y * g.astype(jnp.float32)).astype(x.dtype)


def partial_rope(t, positions):
    half = ROPE_DIM // 2
    inv_freq = ROPE_THETA ** (-jnp.arange(0, ROPE_DIM, 2, dtype=jnp.float32) / ROPE_DIM)
    ang = positions.astype(jnp.float32)[..., None] * inv_freq
    cos = jnp.cos(ang)[:, :, None, None, :]
    sin = jnp.sin(ang)[:, :, None, None, :]
    tf = t.astype(jnp.float32)
    t1 = tf[..., :half]
    t2 = tf[..., half:ROPE_DIM]
    rot = jnp.concatenate([t1 * cos - t2 * sin, t2 * cos + t1 * sin], axis=-1)
    return jnp.concatenate([rot, tf[..., ROPE_DIM:]], axis=-1)


def diff_attention(q, k, v, lam_q1, lam_k1, lam_q2, lam_k2, norm_g, lambda_init):
    bsz, s_len = q.shape[0], q.shape[1]
    scale = ATTN_QKDIM ** -0.5
    qf = jnp.transpose(q, (0, 2, 3, 1, 4)) * scale
    kf = jnp.transpose(k, (0, 2, 3, 1, 4))
    vf = jnp.transpose(v.astype(jnp.float32), (0, 2, 1, 3))
    lam = (jnp.exp(jnp.sum(lam_q1.astype(jnp.float32) * lam_k1.astype(jnp.float32)))
           - jnp.exp(jnp.sum(lam_q2.astype(jnp.float32) * lam_k2.astype(jnp.float32)))
           + lambda_init)
    outs = []
    for blk in range(s_len // Q_BLOCK):
        q0 = blk * Q_BLOCK
        kv_end = q0 + Q_BLOCK
        qb = qf[:, :, :, q0:kv_end]
        kb = kf[:, :, :, :kv_end]
        vb = vf[:, :, :kv_end]
        sc = jnp.einsum('bhmqd,bhmkd->bhmqk', qb, kb)
        mask = jnp.arange(kv_end)[None, :] <= (q0 + jnp.arange(Q_BLOCK))[:, None]
        sc = jnp.where(mask, sc, -jnp.inf)
        p = jax.nn.softmax(sc, axis=-1)
        w = p[:, :, 0] - lam * p[:, :, 1]
        outs.append(jnp.einsum('bhqk,bhkd->bhqd', w, vb))
    o = jnp.concatenate(outs, axis=2)
    o = rmsnorm(o, norm_g) * (1.0 - lambda_init)
    return jnp.transpose(o, (0, 2, 1, 3)).reshape(bsz, s_len, D_ATTN)


def s5_branch(u, lam_re, lam_im, b_re, b_im, c_re, c_im, d_skip, log_dt, w_glu, b_glu, norm_g):
    bsz, s_len, _ = u.shape
    uf = u.astype(jnp.float32)
    ug = uf.reshape(bsz, s_len, SSM_GROUPS, SSM_GROUP)
    dt = jnp.exp(log_dt.astype(jnp.float32))[:, None]
    lam = lax.complex(lam_re.astype(jnp.float32), lam_im.astype(jnp.float32))
    a_bar = jnp.exp(lam * dt)
    b_cplx = lax.complex(b_re.astype(jnp.float32), b_im.astype(jnp.float32))
    b_bar = ((a_bar - 1.0) / lam)[..., None] * b_cplx
    bu = jnp.einsum('bsgc,gnc->bsgn', ug.astype(jnp.complex64), b_bar)
    a_seq = jnp.broadcast_to(a_bar, (1, s_len) + a_bar.shape)

    def combine(left, right):
        a_l, b_l = left
        a_r, b_r = right
        return a_r * a_l, a_r * b_l + b_r

    _, states = lax.associative_scan(combine, (a_seq, bu), axis=1)
    y = (jnp.einsum('bsgn,gcn->bsgc', jnp.real(states), c_re.astype(jnp.float32))
         - jnp.einsum('bsgn,gcn->bsgc', jnp.imag(states), c_im.astype(jnp.float32)))
    y = y.reshape(bsz, s_len, D_SSM) + d_skip.astype(jnp.float32) * uf
    y = jax.nn.gelu(y, approximate=False)
    y = y * jax.nn.sigmoid(y @ w_glu.astype(jnp.float32) + b_glu.astype(jnp.float32))
    return rmsnorm(y, norm_g).astype(u.dtype)


def moe_ffn(h, w_router, b_router, w_gate_up, b_gate_up, w_down, b_down):
    bsz, s_len, d = h.shape
    n_tok = bsz * s_len
    xf = h.reshape(n_tok, d)
    logits = xf.astype(jnp.float32) @ w_router.astype(jnp.float32) + b_router.astype(jnp.float32)
    top_val, top_idx = lax.top_k(logits, TOP_K)
    gates = jax.nn.softmax(top_val, axis=-1)
    n_assign = n_tok * TOP_K
    e_flat = top_idx.reshape(-1)
    tok_flat = jnp.arange(n_assign, dtype=jnp.int32) // TOP_K
    order = jnp.argsort(e_flat)
    e_sorted = e_flat[order]
    tok_sorted = tok_flat[order]
    gate_sorted = gates.reshape(-1)[order]
    counts = jnp.bincount(e_flat, length=N_EXPERTS)
    starts = jnp.cumsum(counts) - counts
    padded = (counts + MOE_BLOCK - 1) // MOE_BLOCK * MOE_BLOCK
    pad_ends = jnp.cumsum(padded)
    pad_starts = pad_ends - padded
    dest = pad_starts[e_sorted] + (jnp.arange(n_assign, dtype=jnp.int32) - starts[e_sorted])
    n_rows = n_assign + N_EXPERTS * MOE_BLOCK
    n_blocks = n_rows // MOE_BLOCK
    x_pad = jnp.zeros((n_rows, d), h.dtype).at[dest].set(xf[tok_sorted])
    blk_expert = jnp.minimum(
        jnp.searchsorted(pad_ends, jnp.arange(n_blocks, dtype=jnp.int32) * MOE_BLOCK, side='right'),
        N_EXPERTS - 1)

    def expert_block(args):
        xb, e = args
        gu = xb @ w_gate_up[e] + b_gate_up[e]
        gate = jnp.minimum(gu[:, 0::2], SWIGLU_LIMIT)
        up = jnp.clip(gu[:, 1::2], -SWIGLU_LIMIT, SWIGLU_LIMIT)
        act = gate * jax.nn.sigmoid(SWIGLU_ALPHA * gate) * (up + 1.0)
        return act @ w_down[e] + b_down[e]

    y_pad = lax.map(expert_block, (x_pad.reshape(n_blocks, MOE_BLOCK, d), blk_expert))
    y_rows = y_pad.reshape(n_rows, d)[dest]
    out = jnp.zeros((n_tok, d), jnp.float32).at[tok_sorted].add(
        y_rows.astype(jnp.float32) * gate_sorted[:, None])
    return out.reshape(bsz, s_len, d).astype(h.dtype)


def setup_inputs(seed: int = 0) -> dict:
    key = jax.random.key(seed)
    ks = jax.random.split(key, 32)
    f32 = jnp.float32
    L, D, G, N, C = DEPTH, D_MODEL, SSM_GROUPS, SSM_STATE, SSM_GROUP
    nrm = lambda k, shp, s: jax.random.normal(k, shp, f32) * s
    x = jax.random.normal(ks[0], (BATCH, SEQ, D), f32)
    offsets = jax.random.randint(ks[1], (BATCH, 1), 0, 1024, dtype=jnp.int32)
    positions = jnp.arange(SEQ, dtype=jnp.int32)[None, :] + offsets
    n_in = 3 * D_ATTN + D_SSM
    lam_im = jnp.broadcast_to(jnp.pi * jnp.arange(N, dtype=f32), (L, G, N))
    return {
        "x": x,
        "positions": positions,
        "ln_mix_g": 1.0 + nrm(ks[2], (L, D), 0.02),
        "w_in": nrm(ks[3], (L, D, n_in), D ** -0.5),
        "lam_q1": nrm(ks[4], (L, ATTN_QKDIM), 0.1),
        "lam_k1": nrm(ks[5], (L, ATTN_QKDIM), 0.1),
        "lam_q2": nrm(ks[6], (L, ATTN_QKDIM), 0.1),
        "lam_k2": nrm(ks[7], (L, ATTN_QKDIM), 0.1),
        "diff_norm_g": 1.0 + nrm(ks[8], (L, ATTN_VDIM), 0.02),
        "ssm_lam_re": -0.5 + nrm(ks[9], (L, G, N), 0.01),
        "ssm_lam_im": lam_im + nrm(ks[10], (L, G, N), 0.01),
        "ssm_b_re": nrm(ks[11], (L, G, N, C), (0.5 / C) ** 0.5),
        "ssm_b_im": nrm(ks[12], (L, G, N, C), (0.5 / C) ** 0.5),
        "ssm_c_re": nrm(ks[13], (L, G, C, N), (0.5 / N) ** 0.5),
        "ssm_c_im": nrm(ks[14], (L, G, C, N), (0.5 / N) ** 0.5),
        "ssm_d": nrm(ks[15], (L, D_SSM), 1.0),
        "ssm_log_dt": jax.random.uniform(ks[16], (L, G), f32, math.log(DT_MIN), math.log(DT_MAX)),
        "ssm_w_glu": nrm(ks[17], (L, D_SSM, D_SSM), D_SSM ** -0.5),
        "ssm_b_glu": nrm(ks[18], (L, D_SSM), 0.01),
        "ssm_norm_g": 1.0 + nrm(ks[19], (L, D_SSM), 0.02),
        "w_out": nrm(ks[20], (L, D_MIX, D), D_MIX ** -0.5),
        "ln_ffn_g": 1.0 + nrm(ks[21], (L, D), 0.02),
        "w_router": nrm(ks[22], (L, D, N_EXPERTS), D ** -0.5),
        "b_router": nrm(ks[23], (L, N_EXPERTS), 0.01),
        "w_gate_up": nrm(ks[24], (L, N_EXPERTS, D, 2 * D_FF), D ** -0.5),
        "b_gate_up": nrm(ks[25], (L, N_EXPERTS, 2 * D_FF), 0.01),
        "w_down": nrm(ks[26], (L, N_EXPERTS, D_FF, D), D_FF ** -0.5),
        "b_down": nrm(ks[27], (L, N_EXPERTS, D), 0.01),
        "final_norm_g": 1.0 + nrm(ks[28], (D,), 0.02),
    }


def reference(x, positions, ln_mix_g, w_in, lam_q1, lam_k1, lam_q2, lam_k2, diff_norm_g,
              ssm_lam_re, ssm_lam_im, ssm_b_re, ssm_b_im, ssm_c_re, ssm_c_im, ssm_d,
              ssm_log_dt, ssm_w_glu, ssm_b_glu, ssm_norm_g, w_out, ln_ffn_g,
              w_router, b_router, w_gate_up, b_gate_up, w_down, b_down, final_norm_g):
    bsz, s_len, _ = x.shape
    h = x
    for li in range(DEPTH):
        lambda_init = 0.8 - 0.6 * math.exp(-0.3 * li)
        n = rmsnorm(h, ln_mix_g[li])
        proj = n @ w_in[li]
        q, k, v, u = jnp.split(proj, [D_ATTN, 2 * D_ATTN, 3 * D_ATTN], axis=-1)
        q = partial_rope(q.reshape(bsz, s_len, ATTN_HEADS, 2, ATTN_QKDIM), positions)
        k = partial_rope(k.reshape(bsz, s_len, ATTN_HEADS, 2, ATTN_QKDIM), positions)
        v = v.reshape(bsz, s_len, ATTN_HEADS, ATTN_VDIM)
        a_out = diff_attention(q, k, v, lam_q1[li], lam_k1[li], lam_q2[li], lam_k2[li],
                               diff_norm_g[li], lambda_init).astype(x.dtype)
        s_out = s5_branch(u, ssm_lam_re[li], ssm_lam_im[li], ssm_b_re[li], ssm_b_im[li],
                          ssm_c_re[li], ssm_c_im[li], ssm_d[li], ssm_log_dt[li],
                          ssm_w_glu[li], ssm_b_glu[li], ssm_norm_g[li])
        h = h + jnp.concatenate([a_out, s_out], axis=-1) @ w_out[li]
        h = h + moe_ffn(rmsnorm(h, ln_ffn_g[li]), w_router[li], b_router[li],
                        w_gate_up[li], b_gate_up[li], w_down[li], b_down[li])
    return rmsnorm(h, final_norm_g)
```

```python
import functools
import math

import jax
import jax.numpy as jnp
from jax import lax
from jax.experimental import pallas as pl
from jax.experimental.pallas import tpu as pltpu

F32 = jnp.float32
BF16 = jnp.bfloat16
I32 = jnp.int32

RMS_EPS = 1e-6
ATTN_HEADS = 4
ROPE_DIM = 16
ROPE_THETA = 500000.0
SSM_GROUP = 16
SSM_STATE = 64
TOP_K = 4
SWIGLU_LIMIT = 7.0
SWIGLU_ALPHA = 1.702
LANES = 128
SUBLANES = 8
NEG_BIG = -1e30

TOKEN_TILE = 512
ATTN_TILE = 512
SSM_CHUNK = 32
SSM_BATCH_PAD = SUBLANES
SCAN_BLOCK = 32
MOE_ROWS = 256
VMEM_LIMIT = 56 * 1024 * 1024


def _rms(x, g):
    return x * lax.rsqrt(jnp.mean(x * x, axis=-1, keepdims=True) + RMS_EPS) * g


def _dot_t(a, b, **kw):
    return lax.dot_general(a, b, (((1,), (1,)), ((), ())), preferred_element_type=F32, **kw)


def _inproj_kernel(x_ref, pos_ref, g_ref, w_ref, freq_ref, q_ref, k_ref, v_ref, u_ref, *, d_attn):
    x = x_ref[...]
    n = _rms(x, g_ref[...]).astype(BF16)
    proj = jnp.dot(n, w_ref[...], preferred_element_type=F32)
    ang = pos_ref[...].astype(F32) * freq_ref[...]
    lane = lax.broadcasted_iota(I32, (1, LANES), 1) & (LANES // 2 - 1)
    half = ROPE_DIM // 2
    first = lane < half
    second = jnp.logical_and(lane >= half, lane < ROPE_DIM)
    cos = jnp.cos(ang)
    sin = jnp.sin(ang)
    cm = jnp.where(lane < ROPE_DIM, cos, 1.0)
    sa = jnp.where(first, -sin, 0.0)
    sb = jnp.where(second, sin, 0.0)

    def rope(t):
        return t * cm + pltpu.roll(t, LANES - half, 1) * sa + pltpu.roll(t, half, 1) * sb

    scale = (LANES // 2) ** -0.5
    for h in range(d_attn // LANES):
        sl = slice(h * LANES, (h + 1) * LANES)
        q_ref[:, sl] = (rope(proj[:, sl]) * scale).astype(BF16)
        k_ref[:, sl] = rope(proj[:, d_attn + h * LANES:d_attn + (h + 1) * LANES]).astype(BF16)
    v_ref[...] = proj[:, 2 * d_attn:3 * d_attn].astype(BF16)
    u_ref[...] = proj[:, 3 * d_attn:].astype(BF16)


def _inproj(x2, pos2, g, w_in, d_attn, d_ssm):
    t, d = x2.shape
    tm = min(TOKEN_TILE, t)
    n_in = w_in.shape[1]
    inv_freq = ROPE_THETA ** (-jnp.arange(0, ROPE_DIM, 2, dtype=F32) / ROPE_DIM)
    lane = jnp.arange(LANES) % (LANES // 2)
    freq = jnp.where(lane < ROPE_DIM, inv_freq[lane % (ROPE_DIM // 2)], 0.0).reshape(1, LANES)
    row = lambda i: (i, 0)
    const = lambda i: (0, 0)
    return pl.pallas_call(
        functools.partial(_inproj_kernel, d_attn=d_attn),
        out_shape=(jax.ShapeDtypeStruct((t, d_attn), BF16),) * 3 + (jax.ShapeDtypeStruct((t, d_ssm), BF16),),
        grid=(t // tm,),
        in_specs=[pl.BlockSpec((tm, d), row), pl.BlockSpec((tm, 1), row), pl.BlockSpec((1, d), const),
                  pl.BlockSpec((d, n_in), const), pl.BlockSpec((1, LANES), const)],
        out_specs=(pl.BlockSpec((tm, d_attn), row),) * 3 + (pl.BlockSpec((tm, d_ssm), row),),
        compiler_params=pltpu.CompilerParams(dimension_semantics=("arbitrary",), vmem_limit_bytes=VMEM_LIMIT),
        name="inproj",
    )(x2, pos2, g.reshape(1, d), w_in.astype(BF16), freq)


def _attn_kernel(q_ref, k_ref, vt_ref, lq1_ref, lk1_ref, lq2_ref, lk2_ref, g_ref, o_ref, acc1, acc2, *,
                 tile, lambda_init):
    qi = pl.program_id(2)
    q = q_ref[...]
    lane = lax.broadcasted_iota(I32, (1, LANES), 1)
    zero = jnp.zeros_like(q)
    q1 = jnp.where(lane < LANES // 2, q, zero)
    q2 = jnp.where(lane >= LANES // 2, q, zero)
    acc1[...] = jnp.zeros_like(acc1)
    acc2[...] = jnp.zeros_like(acc2)
    m0 = jnp.full((1, tile), NEG_BIG, F32)
    l0 = jnp.zeros((1, tile), F32)

    def step(j, carry, masked):
        m1, l1, m2, l2 = carry
        start = pl.multiple_of(j * tile, tile)
        k = k_ref[pl.ds(start, tile), :]
        vt = vt_ref[:, pl.ds(start, tile)]
        if masked:
            kpos = lax.broadcasted_iota(I32, (tile, tile), 0)
            qpos = lax.broadcasted_iota(I32, (tile, tile), 1)
            keep = kpos <= qpos
        out = []
        for qm, m, l, acc in ((q1, m1, l1, acc1), (q2, m2, l2, acc2)):
            s = _dot_t(k, qm)
            if masked:
                s = jnp.where(keep, s, NEG_BIG)
            m_new = jnp.maximum(m, jnp.max(s, axis=0, keepdims=True))
            alpha = jnp.exp(m - m_new)
            p = jnp.exp(s - m_new)
            l_new = alpha * l + jnp.sum(p, axis=0, keepdims=True)
            acc[...] = acc[...] * alpha + jnp.dot(vt, p.astype(BF16), preferred_element_type=F32)
            out += [m_new, l_new]
        return tuple(out)

    carry = lax.fori_loop(0, qi, lambda j, c: step(j, c, False), (m0, l0, m0, l0))
    _, l1, _, l2 = step(qi, carry, True)

    lam = (jnp.exp(jnp.sum(lq1_ref[...] * lk1_ref[...], axis=-1, keepdims=True))
           - jnp.exp(jnp.sum(lq2_ref[...] * lk2_ref[...], axis=-1, keepdims=True)) + lambda_init)
    ot = acc1[...] / l1 - lam * (acc2[...] / l2)
    o = _rms(ot.T, g_ref[...]) * (1.0 - lambda_init)
    o_ref[...] = o.astype(o_ref.dtype)


def _attention(q, k, vt, lq1, lk1, lq2, lk2, norm_g, lambda_init):
    b, s, d_attn = q.shape
    heads = d_attn // LANES
    tile = min(ATTN_TILE, s)
    dk = lq1.shape[-1]
    vec = lambda a: a.reshape(1, -1).astype(F32)
    const = lambda bi, h, qi: (0, 0)
    return pl.pallas_call(
        functools.partial(_attn_kernel, tile=tile, lambda_init=lambda_init),
        out_shape=jax.ShapeDtypeStruct((b, s, d_attn), BF16),
        grid=(b, heads, s // tile),
        in_specs=[pl.BlockSpec((None, tile, LANES), lambda bi, h, qi: (bi, qi, h)),
                  pl.BlockSpec((None, s, LANES), lambda bi, h, qi: (bi, 0, h)),
                  pl.BlockSpec((None, LANES, s), lambda bi, h, qi: (bi, h, 0)),
                  pl.BlockSpec((1, dk), const), pl.BlockSpec((1, dk), const),
                  pl.BlockSpec((1, dk), const), pl.BlockSpec((1, dk), const),
                  pl.BlockSpec((1, LANES), const)],
        out_specs=pl.BlockSpec((None, tile, LANES), lambda bi, h, qi: (bi, qi, h)),
        scratch_shapes=[pltpu.VMEM((LANES, tile), F32), pltpu.VMEM((LANES, tile), F32)],
        compiler_params=pltpu.CompilerParams(dimension_semantics=("arbitrary",) * 3,
                                             vmem_limit_bytes=VMEM_LIMIT),
        name="attn",
    )(q, k, vt, vec(lq1), vec(lk1), vec(lq2), vec(lk2), vec(norm_g))


def _ssm_consts(lre, lim, ldt):
    dt = jnp.exp(ldt)
    return lre * dt, lim * dt


def _cpow(e, ldr, ldi):
    mag = jnp.exp(e * ldr)
    ang = e * ldi
    return mag * jnp.cos(ang), mag * jnp.sin(ang)


def _bbar(lre, lim, ldr, ldi, bre, bim):
    mag = jnp.exp(ldr)
    nr = mag * jnp.cos(ldi) - 1.0
    ni = mag * jnp.sin(ldi)
    den = lre * lre + lim * lim
    cr = (nr * lre + ni * lim) / den
    ci = (ni * lre - nr * lim) / den
    return cr * bre - ci * bim, cr * bim + ci * bre


def _crows(pr, pi, xr, xi):
    nc = xr.shape[0]
    rr = jnp.concatenate([pr * xr[c:c + 1] - pi * xi[c:c + 1] for c in range(nc)], axis=0)
    ri = jnp.concatenate([pr * xi[c:c + 1] + pi * xr[c:c + 1] for c in range(nc)], axis=0)
    return rr, ri


def _s5_local_kernel(u_ref, lre_ref, lim_ref, ldt_ref, bre_ref, bim_ref, zr_ref, zi_ref, *, chunk):
    lane = lax.broadcasted_iota(I32, (1, LANES), 1)
    e = (chunk - 1.0) - lax.broadcasted_iota(I32, (chunk, 1), 0).astype(F32)
    zr = None
    for gi in range(2):
        lre, lim = lre_ref[gi], lim_ref[gi]
        ldr, ldi = _ssm_consts(lre, lim, ldt_ref[gi])
        bbr, bbi = _bbar(lre, lim, ldr, ldi, bre_ref[gi], bim_ref[gi])
        pr, pi = _cpow(e, ldr, ldi)
        wr, wi = _crows(pr, pi, bbr, bbi)
        mine = (lane < LANES // 2) if gi == 0 else (lane >= LANES // 2)
        u = u_ref[gi]
        pr_ = jnp.dot(u, jnp.where(mine, wr, 0.0).astype(BF16), preferred_element_type=F32)
        pi_ = jnp.dot(u, jnp.where(mine, wi, 0.0).astype(BF16), preferred_element_type=F32)
        zr, zi = (pr_, pi_) if zr is None else (zr + pr_, zi + pi_)
    zr_ref[...] = zr
    zi_ref[...] = zi


def _s5_scan_kernel(zr_ref, zi_ref, lre_ref, lim_ref, ldt_ref, xr_ref, xi_ref, sr, si, ar, ai, *, chunk, nblk):
    @pl.when(pl.program_id(0) == 0)
    def _():
        ldr, ldi = _ssm_consts(lre_ref[...], lim_ref[...], ldt_ref[...])
        mag = jnp.exp(chunk * ldr)
        ar[...] = jnp.broadcast_to(mag * jnp.cos(chunk * ldi), ar.shape)
        ai[...] = jnp.broadcast_to(mag * jnp.sin(chunk * ldi), ai.shape)
        sr[...] = jnp.zeros_like(sr)
        si[...] = jnp.zeros_like(si)

    a_r = ar[...]
    a_i = ai[...]

    def body(kk, carry):
        xr, xi = carry
        rows = pl.ds(pl.multiple_of(kk * SSM_BATCH_PAD, SSM_BATCH_PAD), SSM_BATCH_PAD)
        xr_ref[rows, :] = xr
        xi_ref[rows, :] = xi
        return (a_r * xr - a_i * xi + zr_ref[rows, :], a_r * xi + a_i * xr + zi_ref[rows, :])

    xr, xi = lax.fori_loop(0, nblk, body, (sr[...], si[...]))
    sr[...] = xr
    si[...] = xi


def _s5_out_kernel(u_ref, xr_ref, xi_ref, lre_ref, lim_ref, ldt_ref, bre_ref, bim_ref, cre_ref, cim_ref,
                   y_ref, *, chunk):
    lane = lax.broadcasted_iota(I32, (1, LANES), 1)
    low = lane < LANES // 2
    step = lax.broadcasted_iota(I32, (chunk, 1), 0).astype(F32)
    r = u_ref.shape[-1]
    row_t = lax.broadcasted_iota(I32, (r, r), 0) & (chunk - 1)
    col_t = lax.broadcasted_iota(I32, (r, r), 1) & (chunk - 1)
    causal = col_t >= row_t
    xr = xr_ref[...].astype(BF16)
    xi = xi_ref[...].astype(BF16)
    for gi in range(2):
        lre, lim = lre_ref[gi], lim_ref[gi]
        ldr, ldi = _ssm_consts(lre, lim, ldt_ref[gi])
        bbr, bbi = _bbar(lre, lim, ldr, ldi, bre_ref[gi], bim_ref[gi])
        cre, cim = cre_ref[gi], cim_ref[gi]
        nr, ni = _cpow(-step, ldr, ldi)
        br, bi = _crows(nr, ni, bbr, bbi)
        pr, pi = _cpow(step, ldr, ldi)
        gr, gim = _crows(pr, pi, cre, cim)
        toep = _dot_t(jnp.where(low, br, -bi), jnp.where(low, gr, gim), precision=lax.Precision.HIGHEST)
        toep = jnp.where(causal, toep, 0.0).astype(BF16)
        qr, qi = _cpow(step + 1.0, ldr, ldi)
        vr, vi = _crows(qr, qi, cre, cim)
        mine = low if gi == 0 else jnp.logical_not(low)
        y = jnp.dot(u_ref[gi], toep, preferred_element_type=F32)
        y = y + _dot_t(xr, jnp.where(mine, vr, 0.0).astype(BF16))
        y = y - _dot_t(xi, jnp.where(mine, vi, 0.0).astype(BF16))
        y_ref[gi] = y.astype(y_ref.dtype)


def _s5_core(u, bsz, s_len, lam_re, lam_im, b_re, b_im, c_re, c_im, log_dt):
    g, n = lam_re.shape
    nc = SSM_GROUP
    chunk = min(SSM_CHUNK, s_len)
    kc = s_len // chunk
    bp = SSM_BATCH_PAD
    rows = kc * bp
    r = nc * chunk
    assert 2 * n == LANES and g % 2 == 0 and bsz <= bp and kc % min(SCAN_BLOCK, kc) == 0

    uc = u.reshape(bsz, kc, chunk, g, nc)
    uc = jnp.pad(uc, ((0, bp - bsz), (0, 0), (0, 0), (0, 0), (0, 0)))
    uc = uc.transpose(3, 1, 0, 4, 2).reshape(g, rows, r)

    dup = lambda a: jnp.concatenate([a, a], axis=-1).astype(F32)
    lre2 = dup(lam_re)[:, None, :]
    lim2 = dup(lam_im)[:, None, :]
    ldt2 = jnp.broadcast_to(log_dt.astype(F32)[:, None, None], (g, 1, LANES))
    bre2 = dup(b_re.transpose(0, 2, 1))
    bim2 = dup(b_im.transpose(0, 2, 1))
    cre2 = dup(c_re)
    cim2 = dup(c_im)

    pair3 = lambda p: (p, 0, 0)
    vec_spec = pl.BlockSpec((2, 1, LANES), pair3)
    coef_spec = pl.BlockSpec((2, nc, LANES), pair3)
    u_spec = pl.BlockSpec((2, rows, r), pair3)
    z_spec = pl.BlockSpec((rows, LANES), lambda p: (0, p))
    params = pltpu.CompilerParams(dimension_semantics=("arbitrary",), vmem_limit_bytes=VMEM_LIMIT)

    zr, zi = pl.pallas_call(
        functools.partial(_s5_local_kernel, chunk=chunk),
        out_shape=(jax.ShapeDtypeStruct((rows, g * n), F32),) * 2,
        grid=(g // 2,),
        in_specs=[u_spec, vec_spec, vec_spec, vec_spec, coef_spec, coef_spec],
        out_specs=(z_spec, z_spec),
        compiler_params=params, name="s5_local",
    )(uc, lre2, lim2, ldt2, bre2, bim2)

    nblk = min(SCAN_BLOCK, kc)
    flat = lambda a: a.astype(F32).reshape(1, g * n)
    ldt_flat = jnp.broadcast_to(log_dt.astype(F32)[:, None], (g, n)).reshape(1, g * n)
    blk = pl.BlockSpec((nblk * bp, g * n), lambda i: (i, 0))
    cst = pl.BlockSpec((1, g * n), lambda i: (0, 0))
    xr, xi = pl.pallas_call(
        functools.partial(_s5_scan_kernel, chunk=chunk, nblk=nblk),
        out_shape=(jax.ShapeDtypeStruct((rows, g * n), F32),) * 2,
        grid=(kc // nblk,),
        in_specs=[blk, blk, cst, cst, cst],
        out_specs=(blk, blk),
        scratch_shapes=[pltpu.VMEM((bp, g * n), F32)] * 4,
        compiler_params=params, name="s5_scan",
    )(zr, zi, flat(lam_re), flat(lam_im), ldt_flat)

    y = pl.pallas_call(
        functools.partial(_s5_out_kernel, chunk=chunk),
        out_shape=jax.ShapeDtypeStruct((g, rows, r), BF16),
        grid=(g // 2,),
        in_specs=[u_spec, z_spec, z_spec, vec_spec, vec_spec, vec_spec, coef_spec, coef_spec, coef_spec, coef_spec],
        out_specs=u_spec,
        compiler_params=params, name="s5_out",
    )(uc, xr, xi, lre2, lim2, ldt2, bre2, bim2, cre2, cim2)

    y = y.reshape(g, kc, bp, nc, chunk)[:, :, :bsz]
    return y.transpose(2, 1, 4, 0, 3).reshape(bsz * s_len, g * nc)


def _mix_kernel(x_ref, a_ref, y_ref, u_ref, d_ref, wglu_ref, bglu_ref, gs_ref, wout_ref, gffn_ref,
                wr_ref, br_ref, h_ref, hn_ref, idx_ref, gate_ref, rank_ref, cnt_ref, run_ref, *, d_attn):
    i = pl.program_id(0)

    @pl.when(i == 0)
    def _():
        run_ref[...] = jnp.zeros_like(run_ref)

    y = y_ref[...].astype(F32) + d_ref[...] * u_ref[...].astype(F32)
    y = 0.5 * y * (1.0 + lax.erf(y * (0.5 ** 0.5)))
    z = jnp.dot(y.astype(BF16), wglu_ref[...], preferred_element_type=F32) + bglu_ref[...]
    y = y * jax.nn.sigmoid(z)
    s_out = _rms(y, gs_ref[...])
    mix = (jnp.dot(a_ref[...], wout_ref[:d_attn, :], preferred_element_type=F32)
           + jnp.dot(s_out.astype(BF16), wout_ref[d_attn:, :], preferred_element_type=F32))
    h = x_ref[...] + mix
    h_ref[...] = h
    hn = _rms(h, gffn_ref[...])
    for c in range(hn.shape[1] // LANES):
        hn_ref[:, c, :] = hn[:, c * LANES:(c + 1) * LANES]

    tm = hn.shape[0]
    lg = jnp.dot(hn, wr_ref[...], preferred_element_type=F32, precision=lax.Precision.HIGHEST) + br_ref[...]
    lane = lax.broadcasted_iota(I32, (tm, LANES), 1)
    lane_f = lane.astype(F32)
    vals, hots = [], []
    idx_all = jnp.zeros((tm, LANES), F32)
    for kk in range(TOP_K):
        mx = jnp.max(lg, axis=-1, keepdims=True)
        ix = jnp.min(jnp.where(lg == mx, lane_f, float(LANES)), axis=-1, keepdims=True)
        hot = lane_f == ix
        vals.append(mx)
        hots.append(hot)
        idx_all = jnp.where(lane == kk, ix, idx_all)
        lg = jnp.where(hot, -3e38, lg)
    ex = [jnp.exp(v - vals[0]) for v in vals]
    den = ex[0] + ex[1] + ex[2] + ex[3]
    gate_all = jnp.zeros((tm, LANES), F32)
    for kk in range(TOP_K):
        gate_all = jnp.where(lane == kk, ex[kk] / den, gate_all)

    picked = jnp.zeros((tm, LANES), F32)
    for hot in hots:
        picked = jnp.where(hot, 1.0, picked)
    tri = (lax.broadcasted_iota(I32, (tm, tm), 1) < lax.broadcasted_iota(I32, (tm, tm), 0))
    before = jnp.dot(jnp.where(tri, 1.0, 0.0).astype(BF16), picked.astype(BF16), preferred_element_type=F32)
    before = before + run_ref[...]
    rank_all = jnp.zeros((tm, LANES), F32)
    for kk, hot in enumerate(hots):
        rk = jnp.sum(jnp.where(hot, before, 0.0), axis=-1, keepdims=True)
        rank_all = jnp.where(lane == kk, rk, rank_all)
    run_ref[...] = run_ref[...] + jnp.sum(picked, axis=0, keepdims=True)
    cnt_ref[...] = run_ref[...]
    idx_ref[...] = idx_all[:, :TOP_K].astype(I32)
    gate_ref[...] = gate_all[:, :TOP_K]
    rank_ref[...] = rank_all[:, :TOP_K].astype(I32)


def _mix(x2, a_out, y_ssm, u, ssm_d, w_glu, b_glu, ssm_norm_g, w_out, ln_ffn_g, w_router, b_router):
    t, d = x2.shape
    d_attn = a_out.shape[1]
    d_ssm = y_ssm.shape[1]
    n_exp = w_router.shape[1]
    tm = min(TOKEN_TILE, t)
    nch = d // LANES
    wr = jnp.zeros((d, LANES), F32).at[:, :n_exp].set(w_router.astype(F32))
    br = jnp.full((1, LANES), NEG_BIG, F32).at[0, :n_exp].set(b_router.astype(F32))
    row = lambda i: (i, 0)
    const = lambda i: (0, 0)
    vec = lambda a: a.reshape(1, -1).astype(F32)
    return pl.pallas_call(
        functools.partial(_mix_kernel, d_attn=d_attn),
        out_shape=(jax.ShapeDtypeStruct((t, d), F32), jax.ShapeDtypeStruct((t, nch, LANES), F32),
                   jax.ShapeDtypeStruct((t, TOP_K), I32), jax.ShapeDtypeStruct((t, TOP_K), F32),
                   jax.ShapeDtypeStruct((t, TOP_K), I32), jax.ShapeDtypeStruct((1, LANES), F32)),
        grid=(t // tm,),
        in_specs=[pl.BlockSpec((tm, d), row), pl.BlockSpec((tm, d_attn), row), pl.BlockSpec((tm, d_ssm), row),
                  pl.BlockSpec((tm, d_ssm), row), pl.BlockSpec((1, d_ssm), const),
                  pl.BlockSpec((d_ssm, d_ssm), const), pl.BlockSpec((1, d_ssm), const),
                  pl.BlockSpec((1, d_ssm), const), pl.BlockSpec((d_attn + d_ssm, d), const),
                  pl.BlockSpec((1, d), const), pl.BlockSpec((d, LANES), const), pl.BlockSpec((1, LANES), const)],
        out_specs=(pl.BlockSpec((tm, d), row), pl.BlockSpec((tm, nch, LANES), lambda i: (i, 0, 0)),
                   pl.BlockSpec((tm, TOP_K), row), pl.BlockSpec((tm, TOP_K), row), pl.BlockSpec((tm, TOP_K), row),
                   pl.BlockSpec((1, LANES), const)),
        scratch_shapes=[pltpu.VMEM((1, LANES), F32)],
        compiler_params=pltpu.CompilerParams(dimension_semantics=("arbitrary",), vmem_limit_bytes=VMEM_LIMIT),
        name="mix",
    )(x2, a_out, y_ssm, u, vec(ssm_d), w_glu.astype(BF16), vec(b_glu), vec(ssm_norm_g), w_out.astype(BF16),
      vec(ln_ffn_g), wr, br)


def _moe_kernel(blk_e_ref, nused_ref, slot_hbm, hn_hbm, wgu_ref, bgu_ref, wd_ref, bd_ref, ytok_hbm,
                idx_smem, xbuf, ybuf, idx_sem, g_sem, s_sem, *, n_tok, n_blocks, tok_pad, d_ff):
    i = pl.program_id(0)
    n_used = nused_ref[0]
    rows = xbuf.shape[1]
    nch = xbuf.shape[2]

    def idx_copy(blk):
        b = jnp.minimum(blk, n_blocks - 1)
        return pltpu.make_async_copy(slot_hbm.at[b], idx_smem.at[blk % 3], idx_sem.at[blk % 3])

    def gather_wait(blk):
        pltpu.make_async_copy(hn_hbm.at[pl.ds(0, rows)], xbuf.at[blk % 2], g_sem.at[blk % 2]).wait()

    def scatter_wait():
        pltpu.make_async_copy(ybuf, ytok_hbm.at[pl.ds(0, rows)], s_sem.at[0]).wait()

    def issue_gather(blk):
        for r in range(rows):
            tok = jnp.minimum(idx_smem[blk % 3, 0, r] >> 2, n_tok - 1)
            pltpu.make_async_copy(hn_hbm.at[tok], xbuf.at[blk % 2, r], g_sem.at[blk % 2]).start()

    def issue_scatter(blk):
        for r in range(rows):
            s = idx_smem[blk % 3, 0, r]
            dst = (s & 3) * tok_pad + (s >> 2)
            pltpu.make_async_copy(ybuf.at[r], ytok_hbm.at[dst], s_sem.at[0]).start()

    @pl.when(i == 0)
    def _():
        idx_copy(0).start()
        idx_copy(1).start()
        spare = tok_pad - n_tok
        ybuf[...] = jnp.zeros_like(ybuf)
        fills = [pltpu.make_async_copy(ybuf.at[pl.ds(0, spare)], ytok_hbm.at[pl.ds(kk * tok_pad + n_tok, spare)],
                                       s_sem.at[0]) for kk in range(TOP_K)]
        for f in fills:
            f.start()
        for f in fills:
            f.wait()
        idx_copy(0).wait()
        issue_gather(0)

    @pl.when(i < n_used)
    def _():
        idx_copy(i + 1).wait()
        issue_gather(i + 1)
        idx_copy(i + 2).start()
        gather_wait(i)
        x = jnp.concatenate([xbuf[i % 2, :, c, :] for c in range(nch)], axis=-1).astype(BF16)
        gu = jnp.dot(x, wgu_ref[...], preferred_element_type=F32) + bgu_ref[...]
        gate = jnp.minimum(gu[:, :d_ff], SWIGLU_LIMIT)
        up = jnp.clip(gu[:, d_ff:], -SWIGLU_LIMIT, SWIGLU_LIMIT)
        act = gate * jax.nn.sigmoid(SWIGLU_ALPHA * gate) * (up + 1.0)
        y = jnp.dot(act.astype(BF16), wd_ref[...], preferred_element_type=F32) + bd_ref[...]

        @pl.when(i > 0)
        def _():
            scatter_wait()

        for c in range(nch):
            ybuf[:, c, :] = y[:, c * LANES:(c + 1) * LANES]
        issue_scatter(i)

        @pl.when(i == n_used - 1)
        def _():
            scatter_wait()
            gather_wait(i + 1)
            idx_copy(i + 2).wait()


def _moe(hn3, slot_rows, blk_expert, n_used, wgu, bgu, wd, bd, tok_pad):
    n_tok, nch, _ = hn3.shape
    n_blocks, _, rows = slot_rows.shape
    n_exp, d, d_ff2 = wgu.shape
    d_ff = d_ff2 // 2
    wmap = lambda i, be, nu: (be[i], 0, 0)
    return pl.pallas_call(
        functools.partial(_moe_kernel, n_tok=n_tok, n_blocks=n_blocks, tok_pad=tok_pad, d_ff=d_ff),
        out_shape=jax.ShapeDtypeStruct((TOP_K * tok_pad, nch, LANES), F32),
        grid_spec=pltpu.PrefetchScalarGridSpec(
            num_scalar_prefetch=2,
            grid=(n_blocks,),
            in_specs=[pl.BlockSpec(memory_space=pl.ANY), pl.BlockSpec(memory_space=pl.ANY),
                      pl.BlockSpec((None, d, d_ff2), wmap), pl.BlockSpec((None, 1, d_ff2), wmap),
                      pl.BlockSpec((None, d_ff, d), wmap), pl.BlockSpec((None, 1, d), wmap)],
            out_specs=pl.BlockSpec(memory_space=pl.ANY),
            scratch_shapes=[pltpu.SMEM((3, 1, rows), I32), pltpu.VMEM((2, rows, nch, LANES), F32),
                            pltpu.VMEM((rows, nch, LANES), F32), pltpu.SemaphoreType.DMA((3,)),
                            pltpu.SemaphoreType.DMA((2,)), pltpu.SemaphoreType.DMA((1,))]),
        compiler_params=pltpu.CompilerParams(dimension_semantics=("arbitrary",), vmem_limit_bytes=VMEM_LIMIT),
        name="moe",
    )(blk_expert, n_used, slot_rows, hn3, wgu, bgu, wd, bd)


def _combine_kernel(h_ref, y_ref, gate_ref, g_ref, o_ref):
    nch = y_ref.shape[2]
    gates = gate_ref[...]
    parts = []
    ss = None
    for c in range(nch):
        acc = h_ref[:, c * LANES:(c + 1) * LANES]
        for kk in range(TOP_K):
            acc = acc + gates[:, kk:kk + 1] * y_ref[kk, :, c, :]
        parts.append(acc)
        sq = jnp.sum(acc * acc, axis=-1, keepdims=True)
        ss = sq if ss is None else ss + sq
    inv = lax.rsqrt(ss / (nch * LANES) + RMS_EPS)
    for c in range(nch):
        o_ref[:, c * LANES:(c + 1) * LANES] = parts[c] * inv * g_ref[:, c * LANES:(c + 1) * LANES]


def _combine(h, ytok4, gates, final_g):
    t, d = h.shape
    nch = d // LANES
    tm = min(TOKEN_TILE // 2, t)
    return pl.pallas_call(
        _combine_kernel,
        out_shape=jax.ShapeDtypeStruct((t, d), F32),
        grid=(t // tm,),
        in_specs=[pl.BlockSpec((tm, d), lambda i: (i, 0)),
                  pl.BlockSpec((TOP_K, tm, nch, LANES), lambda i: (0, i, 0, 0)),
                  pl.BlockSpec((tm, TOP_K), lambda i: (i, 0)), pl.BlockSpec((1, d), lambda i: (0, 0))],
        out_specs=pl.BlockSpec((tm, d), lambda i: (i, 0)),
        compiler_params=pltpu.CompilerParams(dimension_semantics=("arbitrary",), vmem_limit_bytes=VMEM_LIMIT),
        name="combine",
    )(h, ytok4, gates, final_g.reshape(1, d).astype(F32))


def _moe_ffn(h, hn3, idx, gates, rank, counts, w_gate_up, b_gate_up, w_down, b_down, final_g):
    t, d = h.shape
    n_exp = w_gate_up.shape[0]
    d_ff = w_down.shape[1]
    rows = MOE_ROWS
    n_assign = t * TOP_K
    n_rows = n_assign + n_exp * rows
    n_blocks = n_rows // rows
    tok_pad = t + (2 * rows) // TOP_K

    cnt = counts[0, :n_exp].astype(I32)
    padded = (cnt + rows - 1) // rows * rows
    pad_ends = jnp.cumsum(padded)
    pad_starts = pad_ends - padded
    dest = pad_starts[idx] + rank
    dump = n_assign + jnp.arange(n_rows, dtype=I32) % (2 * rows)
    slot_rows = dump.at[dest.reshape(-1)].set(jnp.arange(n_assign, dtype=I32), unique_indices=True)
    blk_expert = jnp.minimum(
        jnp.searchsorted(pad_ends, jnp.arange(n_blocks, dtype=I32) * rows, side='right'), n_exp - 1).astype(I32)
    n_used = (pad_ends[-1:] // rows).astype(I32)

    wgu = w_gate_up.reshape(n_exp, d, d_ff, 2).transpose(0, 1, 3, 2).reshape(n_exp, d, 2 * d_ff).astype(BF16)
    bgu = b_gate_up.reshape(n_exp, 1, d_ff, 2).transpose(0, 1, 3, 2).reshape(n_exp, 1, 2 * d_ff).astype(F32)
    ytok = _moe(hn3, slot_rows.reshape(n_blocks, 1, rows), blk_expert, n_used, wgu, bgu,
                w_down.astype(BF16), b_down.reshape(n_exp, 1, d).astype(F32), tok_pad)
    ytok4 = ytok.reshape(TOP_K, tok_pad, d // LANES, LANES)
    return _combine(h, ytok4, gates, final_g)


def kernel(x, positions, ln_mix_g, w_in, lam_q1, lam_k1, lam_q2, lam_k2, diff_norm_g, ssm_lam_re, ssm_lam_im,
           ssm_b_re, ssm_b_im, ssm_c_re, ssm_c_im, ssm_d, ssm_log_dt, ssm_w_glu, ssm_b_glu, ssm_norm_g, w_out,
           ln_ffn_g, w_router, b_router, w_gate_up, b_gate_up, w_down, b_down, final_norm_g):
    bsz, s_len, d = x.shape
    depth = w_in.shape[0]
    assert depth == 1
    li = 0
    lambda_init = 0.8 - 0.6 * math.exp(-0.3 * li)
    d_attn = ATTN_HEADS * LANES
    d_ssm = w_in.shape[2] - 3 * d_attn
    t = bsz * s_len

    x2 = x.reshape(t, d)
    q, k, v, u = _inproj(x2, positions.reshape(t, 1), ln_mix_g[li], w_in[li], d_attn, d_ssm)
    vt = v.reshape(bsz, s_len, d_attn).transpose(0, 2, 1)
    a_out = _attention(q.reshape(bsz, s_len, d_attn), k.reshape(bsz, s_len, d_attn), vt,
                       lam_q1[li], lam_k1[li], lam_q2[li], lam_k2[li], diff_norm_g[li], lambda_init)
    y_ssm = _s5_core(u, bsz, s_len, ssm_lam_re[li], ssm_lam_im[li], ssm_b_re[li], ssm_b_im[li],
                     ssm_c_re[li], ssm_c_im[li], ssm_log_dt[li])
    h, hn3, idx, gates, rank, counts = _mix(
        x2, a_out.reshape(t, d_attn), y_ssm, u, ssm_d[li], ssm_w_glu[li], ssm_b_glu[li], ssm_norm_g[li],
        w_out[li], ln_ffn_g[li], w_router[li], b_router[li])
    out = _moe_ffn(h, hn3, idx, gates, rank, counts, w_gate_up[li], b_gate_up[li], w_down[li], b_down[li],
                   final_norm_g)
    return out.reshape(bsz, s_len, d)
```

```python
import functools
import math

import jax
import jax.numpy as jnp
from jax import lax
from jax.experimental import pallas as pl
from jax.experimental.pallas import tpu as pltpu

F32 = jnp.float32
BF16 = jnp.bfloat16
I32 = jnp.int32

RMS_EPS = 1e-6
ATTN_HEADS = 4
ROPE_DIM = 16
ROPE_THETA = 500000.0
SSM_GROUP = 16
SSM_STATE = 64
TOP_K = 4
SWIGLU_LIMIT = 7.0
SWIGLU_ALPHA = 1.702
LANES = 128
SUBLANES = 8
NEG_BIG = -1e30

TOKEN_TILE = 512
ATTN_TILE = 512
SSM_CHUNK = 32
SSM_BATCH_PAD = SUBLANES
SCAN_BLOCK = 32
MOE_ROWS = 256
VMEM_LIMIT = 56 * 1024 * 1024


def _rms(x, g):
    return x * lax.rsqrt(jnp.mean(x * x, axis=-1, keepdims=True) + RMS_EPS) * g


def _dot_t(a, b, **kw):
    return lax.dot_general(a, b, (((1,), (1,)), ((), ())), preferred_element_type=F32, **kw)


def _inproj_kernel(x_ref, pos_ref, g_ref, w_ref, freq_ref, q_ref, k_ref, v_ref, u_ref, *, d_attn):
    x = x_ref[...]
    n = _rms(x, g_ref[...]).astype(BF16)
    proj = jnp.dot(n, w_ref[...], preferred_element_type=F32)
    ang = pos_ref[...].astype(F32) * freq_ref[...]
    lane = lax.broadcasted_iota(I32, (1, LANES), 1) & (LANES // 2 - 1)
    half = ROPE_DIM // 2
    first = lane < half
    second = jnp.logical_and(lane >= half, lane < ROPE_DIM)
    cos = jnp.cos(ang)
    sin = jnp.sin(ang)
    cm = jnp.where(lane < ROPE_DIM, cos, 1.0)
    sa = jnp.where(first, -sin, 0.0)
    sb = jnp.where(second, sin, 0.0)

    def rope(t):
        return t * cm + pltpu.roll(t, LANES - half, 1) * sa + pltpu.roll(t, half, 1) * sb

    scale = (LANES // 2) ** -0.5 * math.log2(math.e)
    for h in range(d_attn // LANES):
        sl = slice(h * LANES, (h + 1) * LANES)
        q_ref[:, sl] = (rope(proj[:, sl]) * scale).astype(BF16)
        k_ref[:, sl] = rope(proj[:, d_attn + h * LANES:d_attn + (h + 1) * LANES]).astype(BF16)
    v_ref[...] = proj[:, 2 * d_attn:3 * d_attn].astype(BF16)
    u_ref[...] = proj[:, 3 * d_attn:].astype(BF16)


def _inproj(x2, pos2, g, w_in, d_attn, d_ssm):
    t, d = x2.shape
    tm = min(TOKEN_TILE, t)
    n_in = w_in.shape[1]
    inv_freq = ROPE_THETA ** (-jnp.arange(0, ROPE_DIM, 2, dtype=F32) / ROPE_DIM)
    lane = jnp.arange(LANES) % (LANES // 2)
    freq = jnp.where(lane < ROPE_DIM, inv_freq[lane % (ROPE_DIM // 2)], 0.0).reshape(1, LANES)
    row = lambda i: (i, 0)
    const = lambda i: (0, 0)
    return pl.pallas_call(
        functools.partial(_inproj_kernel, d_attn=d_attn),
        out_shape=(jax.ShapeDtypeStruct((t, d_attn), BF16),) * 3 + (jax.ShapeDtypeStruct((t, d_ssm), BF16),),
        grid=(t // tm,),
        in_specs=[pl.BlockSpec((tm, d), row), pl.BlockSpec((tm, 1), row), pl.BlockSpec((1, d), const),
                  pl.BlockSpec((d, n_in), const), pl.BlockSpec((1, LANES), const)],
        out_specs=(pl.BlockSpec((tm, d_attn), row),) * 3 + (pl.BlockSpec((tm, d_ssm), row),),
        compiler_params=pltpu.CompilerParams(dimension_semantics=("arbitrary",), vmem_limit_bytes=VMEM_LIMIT),
        name="inproj",
    )(x2, pos2, g.reshape(1, d), w_in.astype(BF16), freq)


def _attn_kernel(q_ref, k_ref, vt_ref, lq1_ref, lk1_ref, lq2_ref, lk2_ref, g_ref, o_ref, acc1, acc2, *,
                 tile, lambda_init):
    qi = pl.program_id(2)
    q = q_ref[...]
    lane = lax.broadcasted_iota(I32, (1, LANES), 1)
    zero = jnp.zeros_like(q)
    q_maps = (jnp.where(lane < LANES // 2, q, zero), jnp.where(lane >= LANES // 2, q, zero))
    accs = (acc1, acc2)
    acc1[...] = jnp.zeros_like(acc1)
    acc2[...] = jnp.zeros_like(acc2)
    m0 = jnp.full((1, tile), NEG_BIG, F32)
    l0 = jnp.zeros((1, tile), F32)

    def step(j, carry, masked):
        start = pl.multiple_of(j * tile, tile)
        k = k_ref[pl.ds(start, tile), :]
        vt = vt_ref[:, pl.ds(start, tile)]
        if masked:
            keep = lax.broadcasted_iota(I32, (tile, tile), 0) <= lax.broadcasted_iota(I32, (tile, tile), 1)
        out = []
        for mi in range(2):
            m, l = carry[2 * mi], carry[2 * mi + 1]
            s = _dot_t(k, q_maps[mi])
            if masked:
                s = jnp.where(keep, s, NEG_BIG)
            m_new = jnp.maximum(m, jnp.max(s, axis=0, keepdims=True))
            alpha = jnp.exp2(m - m_new)
            p = jnp.exp2(s - m_new)
            l_new = alpha * l + jnp.sum(p, axis=0, keepdims=True)
            accs[mi][...] = accs[mi][...] * alpha + jnp.dot(vt, p.astype(BF16), preferred_element_type=F32)
            out += [m_new, l_new]
        return tuple(out)

    carry = lax.fori_loop(0, qi, lambda j, c: step(j, c, False), (m0, l0, m0, l0))
    _, l1, _, l2 = step(qi, carry, True)

    lam = (jnp.exp(jnp.sum(lq1_ref[...] * lk1_ref[...], axis=-1, keepdims=True))
           - jnp.exp(jnp.sum(lq2_ref[...] * lk2_ref[...], axis=-1, keepdims=True)) + lambda_init)
    ot = acc1[...] / l1 - lam * (acc2[...] / l2)
    o = _rms(ot.T, g_ref[...]) * (1.0 - lambda_init)
    o_ref[...] = o.astype(o_ref.dtype)


def _attention(q, k, vt, lq1, lk1, lq2, lk2, norm_g, lambda_init):
    b, s, d_attn = q.shape
    heads = d_attn // LANES
    tile = min(ATTN_TILE, s)
    dk = lq1.shape[-1]
    vec = lambda a: a.reshape(1, -1).astype(F32)
    const = lambda bi, h, qi: (0, 0)
    return pl.pallas_call(
        functools.partial(_attn_kernel, tile=tile, lambda_init=lambda_init),
        out_shape=jax.ShapeDtypeStruct((b, s, d_attn), BF16),
        grid=(b, heads, s // tile),
        in_specs=[pl.BlockSpec((None, tile, LANES), lambda bi, h, qi: (bi, qi, h)),
                  pl.BlockSpec((None, s, LANES), lambda bi, h, qi: (bi, 0, h)),
                  pl.BlockSpec((None, LANES, s), lambda bi, h, qi: (bi, h, 0)),
                  pl.BlockSpec((1, dk), const), pl.BlockSpec((1, dk), const),
                  pl.BlockSpec((1, dk), const), pl.BlockSpec((1, dk), const),
                  pl.BlockSpec((1, LANES), const)],
        out_specs=pl.BlockSpec((None, tile, LANES), lambda bi, h, qi: (bi, qi, h)),
        scratch_shapes=[pltpu.VMEM((LANES, tile), F32), pltpu.VMEM((LANES, tile), F32)],
        compiler_params=pltpu.CompilerParams(dimension_semantics=("arbitrary",) * 3,
                                             vmem_limit_bytes=VMEM_LIMIT),
        name="attn",
    )(q, k, vt, vec(lq1), vec(lk1), vec(lq2), vec(lk2), vec(norm_g))


def _ssm_consts(lre, lim, ldt):
    dt = jnp.exp(ldt)
    return lre * dt, lim * dt


def _cpow(e, ldr, ldi):
    mag = jnp.exp(e * ldr)
    ang = e * ldi
    return mag * jnp.cos(ang), mag * jnp.sin(ang)


def _bbar(lre, lim, ldr, ldi, bre, bim):
    mag = jnp.exp(ldr)
    nr = mag * jnp.cos(ldi) - 1.0
    ni = mag * jnp.sin(ldi)
    den = lre * lre + lim * lim
    cr = (nr * lre + ni * lim) / den
    ci = (ni * lre - nr * lim) / den
    return cr * bre - ci * bim, cr * bim + ci * bre


def _crows(pr, pi, xr, xi):
    nc = xr.shape[0]
    rr = jnp.concatenate([pr * xr[c:c + 1] - pi * xi[c:c + 1] for c in range(nc)], axis=0)
    ri = jnp.concatenate([pr * xi[c:c + 1] + pi * xr[c:c + 1] for c in range(nc)], axis=0)
    return rr, ri


def _s5_local_kernel(u_ref, lre_ref, lim_ref, ldt_ref, bre_ref, bim_ref, zr_ref, zi_ref, *, chunk):
    lane = lax.broadcasted_iota(I32, (1, LANES), 1)
    e = (chunk - 1.0) - lax.broadcasted_iota(I32, (chunk, 1), 0).astype(F32)
    zr = None
    for gi in range(2):
        lre, lim = lre_ref[gi], lim_ref[gi]
        ldr, ldi = _ssm_consts(lre, lim, ldt_ref[gi])
        bbr, bbi = _bbar(lre, lim, ldr, ldi, bre_ref[gi], bim_ref[gi])
        pr, pi = _cpow(e, ldr, ldi)
        wr, wi = _crows(pr, pi, bbr, bbi)
        mine = (lane < LANES // 2) if gi == 0 else (lane >= LANES // 2)
        u = u_ref[gi]
        pr_ = jnp.dot(u, jnp.where(mine, wr, 0.0).astype(BF16), preferred_element_type=F32)
        pi_ = jnp.dot(u, jnp.where(mine, wi, 0.0).astype(BF16), preferred_element_type=F32)
        zr, zi = (pr_, pi_) if zr is None else (zr + pr_, zi + pi_)
    zr_ref[...] = zr
    zi_ref[...] = zi


def _s5_scan_kernel(zr_ref, zi_ref, lre_ref, lim_ref, ldt_ref, xr_ref, xi_ref, sr, si, ar, ai, *, chunk, nblk):
    @pl.when(pl.program_id(0) == 0)
    def _():
        ldr, ldi = _ssm_consts(lre_ref[...], lim_ref[...], ldt_ref[...])
        mag = jnp.exp(chunk * ldr)
        ar[...] = jnp.broadcast_to(mag * jnp.cos(chunk * ldi), ar.shape)
        ai[...] = jnp.broadcast_to(mag * jnp.sin(chunk * ldi), ai.shape)
        sr[...] = jnp.zeros_like(sr)
        si[...] = jnp.zeros_like(si)

    a_r = ar[...]
    a_i = ai[...]

    def body(kk, carry):
        xr, xi = carry
        rows = pl.ds(pl.multiple_of(kk * SSM_BATCH_PAD, SSM_BATCH_PAD), SSM_BATCH_PAD)
        xr_ref[rows, :] = xr
        xi_ref[rows, :] = xi
        return (a_r * xr - a_i * xi + zr_ref[rows, :], a_r * xi + a_i * xr + zi_ref[rows, :])

    xr, xi = lax.fori_loop(0, nblk, body, (sr[...], si[...]))
    sr[...] = xr
    si[...] = xi


def _s5_out_kernel(u_ref, xr_ref, xi_ref, lre_ref, lim_ref, ldt_ref, bre_ref, bim_ref, cre_ref, cim_ref,
                   y_ref, *, chunk):
    lane = lax.broadcasted_iota(I32, (1, LANES), 1)
    low = lane < LANES // 2
    step = lax.broadcasted_iota(I32, (chunk, 1), 0).astype(F32)
    r = u_ref.shape[-1]
    row_t = lax.broadcasted_iota(I32, (r, r), 0) & (chunk - 1)
    col_t = lax.broadcasted_iota(I32, (r, r), 1) & (chunk - 1)
    causal = col_t >= row_t
    xr = xr_ref[...].astype(BF16)
    xi = xi_ref[...].astype(BF16)
    for gi in range(2):
        lre, lim = lre_ref[gi], lim_ref[gi]
        ldr, ldi = _ssm_consts(lre, lim, ldt_ref[gi])
        bbr, bbi = _bbar(lre, lim, ldr, ldi, bre_ref[gi], bim_ref[gi])
        cre, cim = cre_ref[gi], cim_ref[gi]
        nr, ni = _cpow(-step, ldr, ldi)
        br, bi = _crows(nr, ni, bbr, bbi)
        pr, pi = _cpow(step, ldr, ldi)
        gr, gim = _crows(pr, pi, cre, cim)
        toep = _dot_t(jnp.where(low, br, -bi), jnp.where(low, gr, gim), precision=lax.Precision.HIGHEST)
        toep = jnp.where(causal, toep, 0.0).astype(BF16)
        qr, qi = _cpow(step + 1.0, ldr, ldi)
        vr, vi = _crows(qr, qi, cre, cim)
        mine = low if gi == 0 else jnp.logical_not(low)
        y = jnp.dot(u_ref[gi], toep, preferred_element_type=F32)
        y = y + _dot_t(xr, jnp.where(mine, vr, 0.0).astype(BF16))
        y = y - _dot_t(xi, jnp.where(mine, vi, 0.0).astype(BF16))
        y_ref[gi] = y.astype(y_ref.dtype)


def _s5_core(u, bsz, s_len, lam_re, lam_im, b_re, b_im, c_re, c_im, log_dt):
    g, n = lam_re.shape
    nc = SSM_GROUP
    chunk = min(SSM_CHUNK, s_len)
    kc = s_len // chunk
    bp = SSM_BATCH_PAD
    rows = kc * bp
    r = nc * chunk
    assert 2 * n == LANES and g % 2 == 0 and bsz <= bp and kc % min(SCAN_BLOCK, kc) == 0

    uc = u.reshape(bsz, kc, chunk, g, nc)
    uc = jnp.pad(uc, ((0, bp - bsz), (0, 0), (0, 0), (0, 0), (0, 0)))
    uc = uc.transpose(3, 1, 0, 4, 2).reshape(g, rows, r)

    dup = lambda a: jnp.concatenate([a, a], axis=-1).astype(F32)
    lre2 = dup(lam_re)[:, None, :]
    lim2 = dup(lam_im)[:, None, :]
    ldt2 = jnp.broadcast_to(log_dt.astype(F32)[:, None, None], (g, 1, LANES))
    bre2 = dup(b_re.transpose(0, 2, 1))
    bim2 = dup(b_im.transpose(0, 2, 1))
    cre2 = dup(c_re)
    cim2 = dup(c_im)

    pair3 = lambda p: (p, 0, 0)
    vec_spec = pl.BlockSpec((2, 1, LANES), pair3)
    coef_spec = pl.BlockSpec((2, nc, LANES), pair3)
    u_spec = pl.BlockSpec((2, rows, r), pair3)
    z_spec = pl.BlockSpec((rows, LANES), lambda p: (0, p))
    params = pltpu.CompilerParams(dimension_semantics=("arbitrary",), vmem_limit_bytes=VMEM_LIMIT)

    zr, zi = pl.pallas_call(
        functools.partial(_s5_local_kernel, chunk=chunk),
        out_shape=(jax.ShapeDtypeStruct((rows, g * n), F32),) * 2,
        grid=(g // 2,),
        in_specs=[u_spec, vec_spec, vec_spec, vec_spec, coef_spec, coef_spec],
        out_specs=(z_spec, z_spec),
        compiler_params=params, name="s5_local",
    )(uc, lre2, lim2, ldt2, bre2, bim2)

    nblk = min(SCAN_BLOCK, kc)
    flat = lambda a: a.astype(F32).reshape(1, g * n)
    ldt_flat = jnp.broadcast_to(log_dt.astype(F32)[:, None], (g, n)).reshape(1, g * n)
    blk = pl.BlockSpec((nblk * bp, g * n), lambda i: (i, 0))
    cst = pl.BlockSpec((1, g * n), lambda i: (0, 0))
    xr, xi = pl.pallas_call(
        functools.partial(_s5_scan_kernel, chunk=chunk, nblk=nblk),
        out_shape=(jax.ShapeDtypeStruct((rows, g * n), F32),) * 2,
        grid=(kc // nblk,),
        in_specs=[blk, blk, cst, cst, cst],
        out_specs=(blk, blk),
        scratch_shapes=[pltpu.VMEM((bp, g * n), F32)] * 4,
        compiler_params=params, name="s5_scan",
    )(zr, zi, flat(lam_re), flat(lam_im), ldt_flat)

    y = pl.pallas_call(
        functools.partial(_s5_out_kernel, chunk=chunk),
        out_shape=jax.ShapeDtypeStruct((g, rows, r), BF16),
        grid=(g // 2,),
        in_specs=[u_spec, z_spec, z_spec, vec_spec, vec_spec, vec_spec, coef_spec, coef_spec, coef_spec, coef_spec],
        out_specs=u_spec,
        compiler_params=params, name="s5_out",
    )(uc, xr, xi, lre2, lim2, ldt2, bre2, bim2, cre2, cim2)

    y = y.reshape(g, kc, bp, nc, chunk)[:, :, :bsz]
    return y.transpose(2, 1, 4, 0, 3).reshape(bsz * s_len, g * nc)


def _mix_kernel(x_ref, a_ref, y_ref, u_ref, d_ref, wglu_ref, bglu_ref, gs_ref, wout_ref, gffn_ref,
                wr_ref, br_ref, h_ref, hn_ref, idx_ref, gate_ref, rank_ref, cnt_ref, run_ref, *, d_attn):
    i = pl.program_id(0)

    @pl.when(i == 0)
    def _():
        run_ref[...] = jnp.zeros_like(run_ref)

    y = y_ref[...].astype(F32) + d_ref[...] * u_ref[...].astype(F32)
    y = 0.5 * y * (1.0 + lax.erf(y * (0.5 ** 0.5)))
    z = jnp.dot(y.astype(BF16), wglu_ref[...], preferred_element_type=F32) + bglu_ref[...]
    y = y * jax.nn.sigmoid(z)
    s_out = _rms(y, gs_ref[...])
    mix = (jnp.dot(a_ref[...], wout_ref[:d_attn, :], preferred_element_type=F32)
           + jnp.dot(s_out.astype(BF16), wout_ref[d_attn:, :], preferred_element_type=F32))
    h = x_ref[...] + mix
    h_ref[...] = h
    hn = _rms(h, gffn_ref[...])
    tm = hn.shape[0]
    nch = hn.shape[1] // LANES
    for c in range(nch):
        hn_ref[pl.ds(c, tm, stride=nch), :] = hn[:, c * LANES:(c + 1) * LANES]

    lg = jnp.dot(hn, wr_ref[...], preferred_element_type=F32, precision=lax.Precision.HIGHEST) + br_ref[...]
    lane = lax.broadcasted_iota(I32, (tm, LANES), 1)
    lane_f = lane.astype(F32)
    vals, hots = [], []
    idx_all = jnp.zeros((tm, LANES), F32)
    for kk in range(TOP_K):
        mx = jnp.max(lg, axis=-1, keepdims=True)
        ix = jnp.min(jnp.where(lg == mx, lane_f, float(LANES)), axis=-1, keepdims=True)
        hot = lane_f == ix
        vals.append(mx)
        hots.append(hot)
        idx_all = jnp.where(lane == kk, ix, idx_all)
        lg = jnp.where(hot, -3e38, lg)
    ex = [jnp.exp(v - vals[0]) for v in vals]
    den = ex[0] + ex[1] + ex[2] + ex[3]
    gate_all = jnp.zeros((tm, LANES), F32)
    for kk in range(TOP_K):
        gate_all = jnp.where(lane == kk, ex[kk] / den, gate_all)

    picked = jnp.zeros((tm, LANES), F32)
    for hot in hots:
        picked = jnp.where(hot, 1.0, picked)
    tri = (lax.broadcasted_iota(I32, (tm, tm), 1) < lax.broadcasted_iota(I32, (tm, tm), 0))
    before = jnp.dot(jnp.where(tri, 1.0, 0.0).astype(BF16), picked.astype(BF16), preferred_element_type=F32)
    before = before + run_ref[...]
    rank_all = jnp.zeros((tm, LANES), F32)
    for kk, hot in enumerate(hots):
        rk = jnp.sum(jnp.where(hot, before, 0.0), axis=-1, keepdims=True)
        rank_all = jnp.where(lane == kk, rk, rank_all)
    run_ref[...] = run_ref[...] + jnp.sum(picked, axis=0, keepdims=True)
    cnt_ref[...] = run_ref[...]
    idx_ref[...] = idx_all[:, :TOP_K].astype(I32)
    gate_ref[...] = gate_all[:, :TOP_K]
    rank_ref[...] = rank_all[:, :TOP_K].astype(I32)


def _mix(x2, a_out, y_ssm, u, ssm_d, w_glu, b_glu, ssm_norm_g, w_out, ln_ffn_g, w_router, b_router):
    t, d = x2.shape
    d_attn = a_out.shape[1]
    d_ssm = y_ssm.shape[1]
    n_exp = w_router.shape[1]
    tm = min(TOKEN_TILE, t)
    nch = d // LANES
    wr = jnp.zeros((d, LANES), F32).at[:, :n_exp].set(w_router.astype(F32))
    br = jnp.full((1, LANES), NEG_BIG, F32).at[0, :n_exp].set(b_router.astype(F32))
    row = lambda i: (i, 0)
    const = lambda i: (0, 0)
    vec = lambda a: a.reshape(1, -1).astype(F32)
    return pl.pallas_call(
        functools.partial(_mix_kernel, d_attn=d_attn),
        out_shape=(jax.ShapeDtypeStruct((t, d), F32), jax.ShapeDtypeStruct((t * nch, LANES), F32),
                   jax.ShapeDtypeStruct((t, TOP_K), I32), jax.ShapeDtypeStruct((t, TOP_K), F32),
                   jax.ShapeDtypeStruct((t, TOP_K), I32), jax.ShapeDtypeStruct((1, LANES), F32)),
        grid=(t // tm,),
        in_specs=[pl.BlockSpec((tm, d), row), pl.BlockSpec((tm, d_attn), row), pl.BlockSpec((tm, d_ssm), row),
                  pl.BlockSpec((tm, d_ssm), row), pl.BlockSpec((1, d_ssm), const),
                  pl.BlockSpec((d_ssm, d_ssm), const), pl.BlockSpec((1, d_ssm), const),
                  pl.BlockSpec((1, d_ssm), const), pl.BlockSpec((d_attn + d_ssm, d), const),
                  pl.BlockSpec((1, d), const), pl.BlockSpec((d, LANES), const), pl.BlockSpec((1, LANES), const)],
        out_specs=(pl.BlockSpec((tm, d), row), pl.BlockSpec((tm * nch, LANES), row),
                   pl.BlockSpec((tm, TOP_K), row), pl.BlockSpec((tm, TOP_K), row), pl.BlockSpec((tm, TOP_K), row),
                   pl.BlockSpec((1, LANES), const)),
        scratch_shapes=[pltpu.VMEM((1, LANES), F32)],
        compiler_params=pltpu.CompilerParams(dimension_semantics=("arbitrary",), vmem_limit_bytes=VMEM_LIMIT),
        name="mix",
    )(x2, a_out, y_ssm, u, vec(ssm_d), w_glu.astype(BF16), vec(b_glu), vec(ssm_norm_g), w_out.astype(BF16),
      vec(ln_ffn_g), wr, br)


IDX_RING = 4


def _moe_kernel(blk_e_ref, nused_ref, slot_hbm, hn_hbm, wgu_ref, bgu_ref, wd_ref, bd_ref, ytok_hbm,
                idx_smem, xbuf, ybuf, wperm, wdown, idx_sem, g_sem, s_sem, *,
                n_tok, n_blocks, tok_pad, rows, nch):
    i = pl.program_id(0)
    n_used = nused_ref[0]
    blk_rows = rows * nch
    d_ff = wdown.shape[0]
    grp = 2 * LANES

    def idx_copy(blk, src=None):
        src = jnp.minimum(blk, n_blocks - 1) if src is None else src
        slot = blk & (IDX_RING - 1)
        return pltpu.make_async_copy(slot_hbm.at[src], idx_smem.at[slot], idx_sem.at[slot])

    def half(par):
        return pl.ds(par * blk_rows, blk_rows)

    def gather_wait(par):
        pltpu.make_async_copy(hn_hbm.at[pl.ds(0, blk_rows)], xbuf.at[half(par)], g_sem.at[par]).wait()

    def scatter_wait(par):
        pltpu.make_async_copy(ybuf.at[half(par)], ytok_hbm.at[pl.ds(0, blk_rows)], s_sem.at[par]).wait()

    def issue_gather(blk, par):
        slot = blk & (IDX_RING - 1)
        for r in range(rows):
            src = pl.ds(pl.multiple_of(idx_smem[slot, 0, r], nch), nch)
            pltpu.make_async_copy(hn_hbm.at[src], xbuf.at[pl.ds(par * blk_rows + r * nch, nch)],
                                  g_sem.at[par]).start()

    def issue_scatter(blk, par):
        slot = blk & (IDX_RING - 1)
        for r in range(rows):
            dst = pl.ds(pl.multiple_of(idx_smem[slot, 1, r], nch), nch)
            pltpu.make_async_copy(ybuf.at[pl.ds(par * blk_rows + r * nch, nch)], ytok_hbm.at[dst],
                                  s_sem.at[par]).start()

    @pl.when(i == 0)
    def _():
        idx_copy(0).start()
        idx_copy(1).start()
        idx_copy(-1, src=n_blocks - 1).start()
        spare = (tok_pad - n_tok) * nch
        ybuf[...] = jnp.zeros_like(ybuf)
        fills = [pltpu.make_async_copy(ybuf.at[pl.ds(0, spare)],
                                       ytok_hbm.at[pl.ds((kk * tok_pad + n_tok) * nch, spare)], s_sem.at[0])
                 for kk in range(TOP_K)]
        for f in fills:
            f.start()
        for f in fills:
            f.wait()
        idx_copy(0).wait()
        idx_copy(-1, src=n_blocks - 1).wait()
        issue_gather(0, 0)

    def block(par):
        idx_copy(i + 1).wait()
        gather_wait(par)

        @pl.when(i > 0)
        def _():
            scatter_wait(par)

        expert = blk_e_ref[i]

        @pl.when(jnp.logical_or(i == 0, expert != blk_e_ref[jnp.maximum(i - 1, 0)]))
        def _():
            src = lax.broadcasted_iota(I32, (grp, grp), 0)
            col = lax.broadcasted_iota(I32, (grp, grp), 1)
            want = jnp.where(col < LANES, 2 * col, 2 * (col - LANES) + 1)
            perm = jnp.where(src == want, 1.0, 0.0).astype(BF16)
            for gi in range(wperm.shape[1] // grp):
                cols = slice(gi * grp, (gi + 1) * grp)
                wperm[:, cols] = jnp.dot(wgu_ref[:, cols].astype(BF16), perm,
                                         preferred_element_type=F32).astype(BF16)
            wdown[...] = wd_ref[...].astype(BF16)

        issue_gather(i + 1, 1 - par)
        issue_scatter(i - 1, 1 - par)
        x = jnp.concatenate([xbuf[pl.ds(par * blk_rows + c, rows, stride=nch), :] for c in range(nch)],
                            axis=-1).astype(BF16)
        gu = jnp.dot(x, wperm[...], preferred_element_type=F32) + bgu_ref[...]
        acts = []
        for gi in range(d_ff // LANES):
            gate = jnp.minimum(gu[:, gi * grp:gi * grp + LANES], SWIGLU_LIMIT)
            up = jnp.clip(gu[:, gi * grp + LANES:(gi + 1) * grp], -SWIGLU_LIMIT, SWIGLU_LIMIT)
            acts.append((gate * jax.nn.sigmoid(SWIGLU_ALPHA * gate) * (up + 1.0)).astype(BF16))
        y = jnp.dot(jnp.concatenate(acts, axis=-1), wdown[...], preferred_element_type=F32) + bd_ref[...]
        for c in range(nch):
            ybuf[pl.ds(par * blk_rows + c, rows, stride=nch), :] = y[:, c * LANES:(c + 1) * LANES]
        idx_copy(i + 2).start()

        @pl.when(i == n_used - 1)
        def _():
            issue_scatter(i, par)
            scatter_wait(1 - par)
            scatter_wait(par)
            gather_wait(1 - par)
            idx_copy(i + 2).wait()

    for par in range(2):
        pl.when(jnp.logical_and(i < n_used, (i & 1) == par))(functools.partial(block, par))


def _moe(hn2, slot_rows, blk_expert, n_used, wgu, bgu, wd, bd, n_tok, tok_pad):
    n_blocks, _, rows = slot_rows.shape
    n_exp, d, d_ff2 = wgu.shape
    d_ff = d_ff2 // 2
    nch = d // LANES
    wmap = lambda i, be, nu: (be[i], 0, 0)
    return pl.pallas_call(
        functools.partial(_moe_kernel, n_tok=n_tok, n_blocks=n_blocks, tok_pad=tok_pad, rows=rows, nch=nch),
        out_shape=jax.ShapeDtypeStruct((TOP_K * tok_pad * nch, LANES), F32),
        grid_spec=pltpu.PrefetchScalarGridSpec(
            num_scalar_prefetch=2,
            grid=(n_blocks,),
            in_specs=[pl.BlockSpec(memory_space=pl.ANY), pl.BlockSpec(memory_space=pl.ANY),
                      pl.BlockSpec((None, d, d_ff2), wmap), pl.BlockSpec((None, 1, d_ff2), wmap),
                      pl.BlockSpec((None, d_ff, d), wmap), pl.BlockSpec((None, 1, d), wmap)],
            out_specs=pl.BlockSpec(memory_space=pl.ANY),
            scratch_shapes=[pltpu.SMEM((IDX_RING, 2, rows), I32), pltpu.VMEM((2 * rows * nch, LANES), F32),
                            pltpu.VMEM((2 * rows * nch, LANES), F32), pltpu.VMEM((d, d_ff2), BF16),
                            pltpu.VMEM((d_ff, d), BF16), pltpu.SemaphoreType.DMA((IDX_RING,)),
                            pltpu.SemaphoreType.DMA((2,)), pltpu.SemaphoreType.DMA((2,))]),
        compiler_params=pltpu.CompilerParams(dimension_semantics=("arbitrary",), vmem_limit_bytes=VMEM_LIMIT),
        name="moe",
    )(blk_expert, n_used, slot_rows, hn2, wgu, bgu, wd, bd)


def _combine_kernel(h_ref, y_ref, gate_ref, g_ref, o_ref):
    tm, d = h_ref.shape
    nch = d // LANES
    gates = gate_ref[...]
    parts = []
    ss = None
    for c in range(nch):
        acc = h_ref[:, c * LANES:(c + 1) * LANES]
        for kk in range(TOP_K):
            acc = acc + gates[:, kk:kk + 1] * y_ref[kk, pl.ds(c, tm, stride=nch), :]
        parts.append(acc)
        sq = jnp.sum(acc * acc, axis=-1, keepdims=True)
        ss = sq if ss is None else ss + sq
    inv = lax.rsqrt(ss / (nch * LANES) + RMS_EPS)
    for c in range(nch):
        o_ref[:, c * LANES:(c + 1) * LANES] = parts[c] * inv * g_ref[:, c * LANES:(c + 1) * LANES]


def _combine(h, ytok3, gates, final_g):
    t, d = h.shape
    nch = d // LANES
    tm = min(TOKEN_TILE // 2, t)
    return pl.pallas_call(
        _combine_kernel,
        out_shape=jax.ShapeDtypeStruct((t, d), F32),
        grid=(t // tm,),
        in_specs=[pl.BlockSpec((tm, d), lambda i: (i, 0)),
                  pl.BlockSpec((TOP_K, tm * nch, LANES), lambda i: (0, i, 0)),
                  pl.BlockSpec((tm, TOP_K), lambda i: (i, 0)), pl.BlockSpec((1, d), lambda i: (0, 0))],
        out_specs=pl.BlockSpec((tm, d), lambda i: (i, 0)),
        compiler_params=pltpu.CompilerParams(dimension_semantics=("arbitrary",), vmem_limit_bytes=VMEM_LIMIT),
        name="combine",
    )(h, ytok3, gates, final_g.reshape(1, d).astype(F32))


def _moe_ffn(h, hn2, idx, gates, rank, counts, w_gate_up, b_gate_up, w_down, b_down, final_g):
    t, d = h.shape
    n_exp = w_gate_up.shape[0]
    d_ff = w_down.shape[1]
    rows = MOE_ROWS
    n_assign = t * TOP_K
    n_rows = n_assign + n_exp * rows
    n_blocks = n_rows // rows
    tok_pad = t + (2 * rows) // TOP_K

    cnt = counts[0, :n_exp].astype(I32)
    padded = (cnt + rows - 1) // rows * rows
    pad_ends = jnp.cumsum(padded)
    pad_starts = pad_ends - padded
    dest = pad_starts[idx] + rank
    dump = n_assign + jnp.arange(n_rows, dtype=I32) % (2 * rows)
    slot_rows = dump.at[dest.reshape(-1)].set(jnp.arange(n_assign, dtype=I32), unique_indices=True)
    blk_start = jnp.arange(n_blocks, dtype=I32) * rows
    blk_expert = jnp.minimum(jnp.sum((pad_ends[None, :] <= blk_start[:, None]).astype(I32), axis=1), n_exp - 1)
    n_used = (pad_ends[-1:] // rows).astype(I32)

    bgu = (b_gate_up.astype(F32).reshape(n_exp, d_ff // LANES, LANES, 2).transpose(0, 1, 3, 2)
           .reshape(n_exp, 1, 2 * d_ff))
    nch = d // LANES
    src_row = jnp.minimum(slot_rows >> 2, t - 1) * nch
    dst_row = ((slot_rows & (TOP_K - 1)) * tok_pad + (slot_rows >> 2)) * nch
    row_idx = jnp.stack([src_row.reshape(n_blocks, rows), dst_row.reshape(n_blocks, rows)], axis=1)
    ytok = _moe(hn2, row_idx, blk_expert, n_used, w_gate_up, bgu,
                w_down, b_down.reshape(n_exp, 1, d).astype(F32), t, tok_pad)
    return _combine(h, ytok.reshape(TOP_K, tok_pad * (d // LANES), LANES), gates, final_g)


def kernel(x, positions, ln_mix_g, w_in, lam_q1, lam_k1, lam_q2, lam_k2, diff_norm_g, ssm_lam_re, ssm_lam_im,
           ssm_b_re, ssm_b_im, ssm_c_re, ssm_c_im, ssm_d, ssm_log_dt, ssm_w_glu, ssm_b_glu, ssm_norm_g, w_out,
           ln_ffn_g, w_router, b_router, w_gate_up, b_gate_up, w_down, b_down, final_norm_g):
    bsz, s_len, d = x.shape
    depth = w_in.shape[0]
    assert depth == 1
    li = 0
    lambda_init = 0.8 - 0.6 * math.exp(-0.3 * li)
    d_attn = ATTN_HEADS * LANES
    d_ssm = w_in.shape[2] - 3 * d_attn
    t = bsz * s_len

    x2 = x.reshape(t, d)
    q, k, v, u = _inproj(x2, positions.reshape(t, 1), ln_mix_g[li], w_in[li], d_attn, d_ssm)
    vt = v.reshape(bsz, s_len, d_attn).transpose(0, 2, 1)
    a_out = _attention(q.reshape(bsz, s_len, d_attn), k.reshape(bsz, s_len, d_attn), vt,
                       lam_q1[li], lam_k1[li], lam_q2[li], lam_k2[li], diff_norm_g[li], lambda_init)
    y_ssm = _s5_core(u, bsz, s_len, ssm_lam_re[li], ssm_lam_im[li], ssm_b_re[li], ssm_b_im[li],
                     ssm_c_re[li], ssm_c_im[li], ssm_log_dt[li])
    h, hn2, idx, gates, rank, counts = _mix(
        x2, a_out.reshape(t, d_attn), y_ssm, u, ssm_d[li], ssm_w_glu[li], ssm_b_glu[li], ssm_norm_g[li],
        w_out[li], ln_ffn_g[li], w_router[li], b_router[li])
    out = _moe_ffn(h, hn2, idx, gates, rank, counts, w_gate_up[li], b_gate_up[li], w_down[li], b_down[li],
                   final_norm_g)
    return out.reshape(bsz, s_len, d)
```

```python
import functools
import math

import jax
import jax.numpy as jnp
from jax import lax
from jax.experimental import pallas as pl
from jax.experimental.pallas import tpu as pltpu

F32 = jnp.float32
BF16 = jnp.bfloat16
I32 = jnp.int32

RMS_EPS = 1e-6
ATTN_HEADS = 4
ROPE_DIM = 16
ROPE_THETA = 500000.0
SSM_GROUP = 16
SSM_STATE = 64
TOP_K = 4
SWIGLU_LIMIT = 7.0
SWIGLU_ALPHA = 1.702
LANES = 128
SUBLANES = 8
NEG_BIG = -1e30

TOKEN_TILE = 1024
MIX_SUBTILE = 512
ATTN_TILE = 512
ATTN_QUERY_SPLITS = 1
ATTN_KEY_SPLITS = 1
SHIFT_SLACK = 64.0
SSM_BATCH_PAD = SUBLANES
SCAN_BLOCK = 32
MOE_ROWS = 256
VMEM_LIMIT = 56 * 1024 * 1024


def _rms(x, g):
    return x * lax.rsqrt(jnp.mean(x * x, axis=-1, keepdims=True) + RMS_EPS) * g


def _dot_t(a, b, **kw):
    return lax.dot_general(a, b, (((1,), (1,)), ((), ())), preferred_element_type=F32, **kw)


def _inproj_kernel(x_ref, pos_ref, g_ref, w_ref, freq_ref, q_ref, k_ref, v_ref, u_ref, ut_ref, *, d_attn):
    x = x_ref[...]
    n = _rms(x, g_ref[...]).astype(BF16)
    proj = jnp.dot(n, w_ref[...], preferred_element_type=F32)
    ang = pos_ref[...].astype(F32) * freq_ref[...]
    lane = lax.broadcasted_iota(I32, (1, LANES), 1) & (LANES // 2 - 1)
    half = ROPE_DIM // 2
    first = lane < half
    second = jnp.logical_and(lane >= half, lane < ROPE_DIM)
    cos = jnp.cos(ang)
    sin = jnp.sin(ang)
    cm = jnp.where(lane < ROPE_DIM, cos, 1.0)
    sa = jnp.where(first, -sin, 0.0)
    sb = jnp.where(second, sin, 0.0)

    def rope(t):
        return t * cm + pltpu.roll(t, LANES - half, 1) * sa + pltpu.roll(t, half, 1) * sb

    scale = (LANES // 2) ** -0.5 * math.log2(math.e)
    for h in range(d_attn // LANES):
        sl = slice(h * LANES, (h + 1) * LANES)
        q_ref[:, sl] = (rope(proj[:, sl]) * scale).astype(BF16)
        k_ref[:, sl] = rope(proj[:, d_attn + h * LANES:d_attn + (h + 1) * LANES]).astype(BF16)
    v_ref[...] = proj[:, 2 * d_attn:3 * d_attn].astype(BF16)
    u = proj[:, 3 * d_attn:]
    u_ref[...] = u.astype(BF16)
    ut = u.T
    for c in range(ut_ref.shape[1]):
        ut_ref[:, c, :] = ut[:, c * LANES:(c + 1) * LANES]


def _inproj(x2, pos2, g, w_in, d_attn, d_ssm, bsz):
    t, d = x2.shape
    s_len = t // bsz
    tm = min(TOKEN_TILE, s_len)
    tiles = s_len // tm
    cpt = tm // LANES
    assert s_len % tm == 0 and cpt % SUBLANES == 0
    n_in = w_in.shape[1]
    inv_freq = ROPE_THETA ** (-jnp.arange(0, ROPE_DIM, 2, dtype=F32) / ROPE_DIM)
    lane = jnp.arange(LANES) % (LANES // 2)
    freq = jnp.where(lane < ROPE_DIM, inv_freq[lane % (ROPE_DIM // 2)], 0.0).reshape(1, LANES)
    row = lambda i: (i, 0)
    const = lambda i: (0, 0)
    return pl.pallas_call(
        functools.partial(_inproj_kernel, d_attn=d_attn),
        out_shape=(jax.ShapeDtypeStruct((t, d_attn), BF16),) * 3 + (
            jax.ShapeDtypeStruct((t, d_ssm), BF16), jax.ShapeDtypeStruct((bsz, d_ssm, s_len // LANES, LANES), F32)),
        grid=(t // tm,),
        in_specs=[pl.BlockSpec((tm, d), row), pl.BlockSpec((tm, 1), row), pl.BlockSpec((1, d), const),
                  pl.BlockSpec((d, n_in), const), pl.BlockSpec((1, LANES), const)],
        out_specs=(pl.BlockSpec((tm, d_attn), row),) * 3 + (
            pl.BlockSpec((tm, d_ssm), row),
            pl.BlockSpec((None, d_ssm, cpt, LANES), lambda i: (i // tiles, 0, i % tiles, 0))),
        compiler_params=pltpu.CompilerParams(dimension_semantics=("arbitrary",), vmem_limit_bytes=VMEM_LIMIT),
        name="inproj",
    )(x2, pos2, g.reshape(1, d), w_in.astype(BF16), freq)


def _attn_kernel(q_ref, k_ref, vt_ref, lq1_ref, lk1_ref, lq2_ref, lk2_ref, g_ref, o_ref, acc1, acc2, *,
                 tile, lambda_init):
    qi = pl.program_id(2)
    q = q_ref[...]
    lane = lax.broadcasted_iota(I32, (1, LANES), 1)
    zero = jnp.zeros_like(q)
    q_maps = (jnp.where(lane < LANES // 2, q, zero), jnp.where(lane >= LANES // 2, q, zero))
    accs = (acc1, acc2)
    acc1[...] = jnp.zeros_like(acc1)
    acc2[...] = jnp.zeros_like(acc2)
    half = tile // ATTN_QUERY_SPLITS
    units = [(mi, h) for mi in range(2) for h in range(ATTN_QUERY_SPLITS)]
    m0 = jnp.full((1, half), NEG_BIG, F32)
    l0 = jnp.zeros((1, half), F32)

    def scores(j, unit, masked):
        mi, h = unit
        k = k_ref[pl.ds(pl.multiple_of(j * tile, tile), tile), :]
        s = _dot_t(k, q_maps[mi][h * half:(h + 1) * half])
        if masked:
            keep = (lax.broadcasted_iota(I32, (tile, half), 0)
                    <= lax.broadcasted_iota(I32, (tile, half), 1) + h * half)
            s = jnp.where(keep, s, NEG_BIG)
        return s

    def values_t(j):
        return vt_ref[:, pl.ds(pl.multiple_of(j * tile, tile), tile)]

    def rescaling_step(j, carry, masked):
        vt = values_t(j)
        out = []
        for ui, (mi, h) in enumerate(units):
            m, l = carry[2 * ui], carry[2 * ui + 1]
            cols = slice(h * half, (h + 1) * half)
            s = scores(j, (mi, h), masked)
            m_new = jnp.maximum(m, jnp.max(s, axis=0, keepdims=True))
            alpha = jnp.exp2(m - m_new)
            p = jnp.exp2(s - m_new)
            l_new = alpha * l + jnp.sum(p, axis=0, keepdims=True)
            accs[mi][:, cols] = (accs[mi][:, cols] * alpha
                                 + jnp.dot(vt, p.astype(BF16), preferred_element_type=F32))
            out += [m_new, l_new]
        return tuple(out)

    def step(j, carry, masked):
        pend = []
        worst = None
        kpart = tile // ATTN_KEY_SPLITS
        for ui, (mi, h) in enumerate(units):
            m = carry[2 * ui]
            pv = lsum = top = None
            for kp in range(ATTN_KEY_SPLITS):
                start = pl.multiple_of(j * tile + kp * kpart, kpart)
                s = _dot_t(k_ref[pl.ds(start, kpart), :], q_maps[mi][h * half:(h + 1) * half])
                if masked:
                    keep = (lax.broadcasted_iota(I32, (kpart, half), 0) + kp * kpart
                            <= lax.broadcasted_iota(I32, (kpart, half), 1) + h * half)
                    s = jnp.where(keep, s, NEG_BIG)
                p = jnp.exp2(s - m)
                part = (jnp.dot(vt_ref[:, pl.ds(start, kpart)], p.astype(BF16), preferred_element_type=F32),
                        jnp.sum(p, axis=0, keepdims=True), jnp.max(s, axis=0, keepdims=True))
                pv, lsum, top = part if pv is None else (pv + part[0], lsum + part[1], jnp.maximum(top, part[2]))
            over = top - m
            worst = over if worst is None else jnp.maximum(worst, over)
            pend += [pv, lsum]

        def keep_shift():
            out = []
            for ui, (mi, h) in enumerate(units):
                cols = slice(h * half, (h + 1) * half)
                accs[mi][:, cols] = accs[mi][:, cols] + pend[2 * ui]
                out += [carry[2 * ui], carry[2 * ui + 1] + pend[2 * ui + 1]]
            return tuple(out)

        return lax.cond(jnp.max(worst) <= SHIFT_SLACK, keep_shift, lambda: rescaling_step(j, carry, masked))

    init = (m0, l0) * len(units)
    carry = lax.cond(qi == 0, lambda: rescaling_step(0, init, True), lambda: rescaling_step(0, init, False))
    carry = lax.fori_loop(1, qi, lambda j, c: step(j, c, False), carry)
    carry = lax.cond(qi > 0, lambda: step(qi, carry, True), lambda: carry)
    ns = ATTN_QUERY_SPLITS
    l1 = jnp.concatenate([carry[2 * ui + 1] for ui in range(ns)], axis=-1)
    l2 = jnp.concatenate([carry[2 * ui + 1] for ui in range(ns, 2 * ns)], axis=-1)

    lam = (jnp.exp(jnp.sum(lq1_ref[...] * lk1_ref[...], axis=-1, keepdims=True))
           - jnp.exp(jnp.sum(lq2_ref[...] * lk2_ref[...], axis=-1, keepdims=True)) + lambda_init)
    ot = acc1[...] / l1 - lam * (acc2[...] / l2)
    o = _rms(ot.T, g_ref[...]) * (1.0 - lambda_init)
    o_ref[...] = o.astype(o_ref.dtype)


def _attention(q, k, vt, lq1, lk1, lq2, lk2, norm_g, lambda_init):
    b, s, d_attn = q.shape
    heads = d_attn // LANES
    tile = min(ATTN_TILE, s)
    dk = lq1.shape[-1]
    vec = lambda a: a.reshape(1, -1).astype(F32)
    const = lambda bi, h, qi: (0, 0)
    return pl.pallas_call(
        functools.partial(_attn_kernel, tile=tile, lambda_init=lambda_init),
        out_shape=jax.ShapeDtypeStruct((b, s, d_attn), BF16),
        grid=(b, heads, s // tile),
        in_specs=[pl.BlockSpec((None, tile, LANES), lambda bi, h, qi: (bi, qi, h)),
                  pl.BlockSpec((None, s, LANES), lambda bi, h, qi: (bi, 0, h)),
                  pl.BlockSpec((None, LANES, s), lambda bi, h, qi: (bi, h, 0)),
                  pl.BlockSpec((1, dk), const), pl.BlockSpec((1, dk), const),
                  pl.BlockSpec((1, dk), const), pl.BlockSpec((1, dk), const),
                  pl.BlockSpec((1, LANES), const)],
        out_specs=pl.BlockSpec((None, tile, LANES), lambda bi, h, qi: (bi, qi, h)),
        scratch_shapes=[pltpu.VMEM((LANES, tile), F32), pltpu.VMEM((LANES, tile), F32)],
        compiler_params=pltpu.CompilerParams(dimension_semantics=("arbitrary",) * 3,
                                             vmem_limit_bytes=VMEM_LIMIT),
        name="attn",
    )(q, k, vt, vec(lq1), vec(lk1), vec(lq2), vec(lk2), vec(norm_g))


def _ssm_consts(lre, lim, ldt):
    dt = jnp.exp(ldt)
    return lre * dt, lim * dt


def _cpow(e, ldr, ldi):
    mag = jnp.exp(e * ldr)
    ang = e * ldi
    return mag * jnp.cos(ang), mag * jnp.sin(ang)


def _bbar(lre, lim, ldr, ldi, bre, bim):
    mag = jnp.exp(ldr)
    nr = mag * jnp.cos(ldi) - 1.0
    ni = mag * jnp.sin(ldi)
    den = lre * lre + lim * lim
    cr = (nr * lre + ni * lim) / den
    ci = (ni * lre - nr * lim) / den
    return cr * bre - ci * bim, cr * bim + ci * bre


def _crows(pr, pi, xr, xi):
    nc = xr.shape[0]
    rr = jnp.concatenate([pr * xr[c:c + 1] - pi * xi[c:c + 1] for c in range(nc)], axis=0)
    ri = jnp.concatenate([pr * xi[c:c + 1] + pi * xr[c:c + 1] for c in range(nc)], axis=0)
    return rr, ri


def _chunk_rows(ut_ref, gi, uc3):
    bsz, nc2, kc, _ = ut_ref.shape
    nc = nc2 // 2
    for b in range(bsz):
        for c in range(nc):
            uc3[:, b, c * LANES:(c + 1) * LANES] = ut_ref[b, gi * nc + c]
    return uc3[...].reshape(kc * uc3.shape[1], nc * LANES).astype(BF16)


def _s5_local_kernel(ut_ref, lre_ref, lim_ref, ldt_ref, bre_ref, bim_ref, zr_ref, zi_ref, uc3, *, chunk):
    @pl.when(pl.program_id(0) == 0)
    def _():
        uc3[...] = jnp.zeros_like(uc3)

    lane = lax.broadcasted_iota(I32, (1, LANES), 1)
    e = (chunk - 1.0) - lax.broadcasted_iota(I32, (chunk, 1), 0).astype(F32)
    zr = None
    for gi in range(2):
        lre, lim = lre_ref[gi], lim_ref[gi]
        ldr, ldi = _ssm_consts(lre, lim, ldt_ref[gi])
        bbr, bbi = _bbar(lre, lim, ldr, ldi, bre_ref[gi], bim_ref[gi])
        pr, pi = _cpow(e, ldr, ldi)
        wr, wi = _crows(pr, pi, bbr, bbi)
        mine = (lane < LANES // 2) if gi == 0 else (lane >= LANES // 2)
        u = _chunk_rows(ut_ref, gi, uc3)
        pr_ = jnp.dot(u, jnp.where(mine, wr, 0.0).astype(BF16), preferred_element_type=F32)
        pi_ = jnp.dot(u, jnp.where(mine, wi, 0.0).astype(BF16), preferred_element_type=F32)
        zr, zi = (pr_, pi_) if zr is None else (zr + pr_, zi + pi_)
    zr_ref[...] = zr
    zi_ref[...] = zi


def _s5_scan_kernel(zr_ref, zi_ref, lre_ref, lim_ref, ldt_ref, xr_ref, xi_ref, sr, si, ar, ai, *, chunk, nblk):
    @pl.when(pl.program_id(0) == 0)
    def _():
        ldr, ldi = _ssm_consts(lre_ref[...], lim_ref[...], ldt_ref[...])
        mag = jnp.exp(chunk * ldr)
        ar[...] = jnp.broadcast_to(mag * jnp.cos(chunk * ldi), ar.shape)
        ai[...] = jnp.broadcast_to(mag * jnp.sin(chunk * ldi), ai.shape)
        sr[...] = jnp.zeros_like(sr)
        si[...] = jnp.zeros_like(si)

    a_r = ar[...]
    a_i = ai[...]

    def body(kk, carry):
        xr, xi = carry
        rows = pl.ds(pl.multiple_of(kk * SSM_BATCH_PAD, SSM_BATCH_PAD), SSM_BATCH_PAD)
        xr_ref[rows, :] = xr
        xi_ref[rows, :] = xi
        return (a_r * xr - a_i * xi + zr_ref[rows, :], a_r * xi + a_i * xr + zi_ref[rows, :])

    xr, xi = lax.fori_loop(0, nblk, body, (sr[...], si[...]))
    sr[...] = xr
    si[...] = xi


def _s5_out_kernel(ut_ref, xr_ref, xi_ref, lre_ref, lim_ref, ldt_ref, bre_ref, bim_ref, cre_ref, cim_ref,
                   yt_ref, uc3, y3, toep, kern, *, chunk):
    @pl.when(pl.program_id(0) == 0)
    def _():
        uc3[...] = jnp.zeros_like(uc3)

    bsz, nc2, kc, _ = ut_ref.shape
    nc = nc2 // 2
    lane = lax.broadcasted_iota(I32, (1, LANES), 1)
    low = lane < LANES // 2
    step = lax.broadcasted_iota(I32, (chunk, 1), 0).astype(F32)
    causal = lax.broadcasted_iota(I32, (chunk, chunk), 1) >= lax.broadcasted_iota(I32, (chunk, chunk), 0)
    xr = xr_ref[...].astype(BF16)
    xi = xi_ref[...].astype(BF16)
    for gi in range(2):
        lre, lim = lre_ref[gi], lim_ref[gi]
        ldr, ldi = _ssm_consts(lre, lim, ldt_ref[gi])
        bbr, bbi = _bbar(lre, lim, ldr, ldi, bre_ref[gi], bim_ref[gi])
        cre, cim = cre_ref[gi], cim_ref[gi]
        pr, pi = _cpow(step, ldr, ldi)
        gr, gim = _crows(pr, pi, cre, cim)
        kmat = _dot_t(jnp.where(low, bbr, -bbi), jnp.where(low, gr, gim), precision=lax.Precision.HIGHEST)
        for c1 in range(nc):
            kern[c1] = kmat[c1:c1 + 1]

        def toeplitz_rows(c1, _):
            rows = pl.ds(pl.multiple_of(c1 * chunk, chunk), chunk)
            for c in range(nc):
                resp = jnp.broadcast_to(kern[c1, :, c * chunk:(c + 1) * chunk], (chunk, chunk))
                blk = pltpu.roll(resp, 0, 1, stride=1, stride_axis=0)
                toep[rows, c * chunk:(c + 1) * chunk] = jnp.where(causal, blk, 0.0).astype(BF16)
            return 0

        lax.fori_loop(0, nc, toeplitz_rows, 0)
        qr, qi = _cpow(step + 1.0, ldr, ldi)
        vr, vi = _crows(qr, qi, cre, cim)
        mine = low if gi == 0 else jnp.logical_not(low)
        y = jnp.dot(_chunk_rows(ut_ref, gi, uc3), toep[...], preferred_element_type=F32)
        y = y + _dot_t(xr, jnp.where(mine, vr, 0.0).astype(BF16))
        y = y - _dot_t(xi, jnp.where(mine, vi, 0.0).astype(BF16))
        y3[...] = y.reshape(y3.shape)
        for b in range(bsz):
            for c in range(nc):
                yt_ref[b, gi * nc + c] = y3[:, b, c * LANES:(c + 1) * LANES]


def _s5_core(ut4, lam_re, lam_im, b_re, b_im, c_re, c_im, log_dt):
    bsz, _, kc, chunk = ut4.shape
    g, n = lam_re.shape
    nc = SSM_GROUP
    bp = SSM_BATCH_PAD
    rows = kc * bp
    r = nc * chunk
    assert 2 * n == LANES and chunk == LANES and g % 2 == 0 and bsz <= bp and kc % min(SCAN_BLOCK, kc) == 0

    dup = lambda a: jnp.concatenate([a, a], axis=-1).astype(F32)
    lre2 = dup(lam_re)[:, None, :]
    lim2 = dup(lam_im)[:, None, :]
    ldt2 = jnp.broadcast_to(log_dt.astype(F32)[:, None, None], (g, 1, LANES))
    bre2 = dup(b_re.transpose(0, 2, 1))
    bim2 = dup(b_im.transpose(0, 2, 1))
    cre2 = dup(c_re)
    cim2 = dup(c_im)

    pair3 = lambda p: (p, 0, 0)
    vec_spec = pl.BlockSpec((2, 1, LANES), pair3)
    coef_spec = pl.BlockSpec((2, nc, LANES), pair3)
    u_spec = pl.BlockSpec((bsz, 2 * nc, kc, chunk), lambda p: (0, p, 0, 0))
    z_spec = pl.BlockSpec((rows, LANES), lambda p: (0, p))
    params = pltpu.CompilerParams(dimension_semantics=("arbitrary",), vmem_limit_bytes=VMEM_LIMIT)
    rows3 = pltpu.VMEM((kc, bp, r), F32)

    zr, zi = pl.pallas_call(
        functools.partial(_s5_local_kernel, chunk=chunk),
        out_shape=(jax.ShapeDtypeStruct((rows, g * n), F32),) * 2,
        grid=(g // 2,),
        in_specs=[u_spec, vec_spec, vec_spec, vec_spec, coef_spec, coef_spec],
        out_specs=(z_spec, z_spec),
        scratch_shapes=[rows3],
        compiler_params=params, name="s5_local",
    )(ut4, lre2, lim2, ldt2, bre2, bim2)

    nblk = min(SCAN_BLOCK, kc)
    flat = lambda a: a.astype(F32).reshape(1, g * n)
    ldt_flat = jnp.broadcast_to(log_dt.astype(F32)[:, None], (g, n)).reshape(1, g * n)
    blk = pl.BlockSpec((nblk * bp, g * n), lambda i: (i, 0))
    cst = pl.BlockSpec((1, g * n), lambda i: (0, 0))
    xr, xi = pl.pallas_call(
        functools.partial(_s5_scan_kernel, chunk=chunk, nblk=nblk),
        out_shape=(jax.ShapeDtypeStruct((rows, g * n), F32),) * 2,
        grid=(kc // nblk,),
        in_specs=[blk, blk, cst, cst, cst],
        out_specs=(blk, blk),
        scratch_shapes=[pltpu.VMEM((bp, g * n), F32)] * 4,
        compiler_params=params, name="s5_scan",
    )(zr, zi, flat(lam_re), flat(lam_im), ldt_flat)

    return pl.pallas_call(
        functools.partial(_s5_out_kernel, chunk=chunk),
        out_shape=jax.ShapeDtypeStruct(ut4.shape, F32),
        grid=(g // 2,),
        in_specs=[u_spec, z_spec, z_spec, vec_spec, vec_spec, vec_spec, coef_spec, coef_spec, coef_spec, coef_spec],
        out_specs=u_spec,
        scratch_shapes=[rows3, rows3, pltpu.VMEM((r, r), BF16), pltpu.VMEM((nc, 1, r), F32)],
        compiler_params=params, name="s5_out",
    )(ut4, xr, xi, lre2, lim2, ldt2, bre2, bim2, cre2, cim2)


def _mix_kernel(x_ref, a_ref, y_ref, u_ref, d_ref, wglu_ref, bglu_ref, gs_ref, wout_ref, gffn_ref,
                wr_ref, br_ref, h_ref, hn_ref, idx_ref, gate_ref, rank_ref, cnt_ref, run_ref, *, d_attn):
    i = pl.program_id(0)

    @pl.when(i == 0)
    def _():
        run_ref[...] = jnp.zeros_like(run_ref)

    tm = min(MIX_SUBTILE, x_ref.shape[0])
    nch = x_ref.shape[1] // LANES
    for sub in range(x_ref.shape[0] // tm):
        _mix_subtile(sub, tm, nch, d_attn, x_ref, a_ref, y_ref, u_ref, d_ref, wglu_ref, bglu_ref, gs_ref,
                     wout_ref, gffn_ref, wr_ref, br_ref, h_ref, hn_ref, idx_ref, gate_ref, rank_ref, run_ref)
    cnt_ref[...] = run_ref[...]


def _mix_subtile(sub, tm, nch, d_attn, x_ref, a_ref, y_ref, u_ref, d_ref, wglu_ref, bglu_ref, gs_ref, wout_ref,
                 gffn_ref, wr_ref, br_ref, h_ref, hn_ref, idx_ref, gate_ref, rank_ref, run_ref):
    rows = pl.ds(sub * tm, tm)
    cps = tm // LANES
    yt = jnp.concatenate([y_ref[:, sub * cps + c, :] for c in range(cps)], axis=-1)
    y = yt.T + d_ref[...] * u_ref[rows, :].astype(F32)
    y = 0.5 * y * (1.0 + lax.erf(y * (0.5 ** 0.5)))
    z = jnp.dot(y.astype(BF16), wglu_ref[...], preferred_element_type=F32) + bglu_ref[...]
    y = y * jax.nn.sigmoid(z)
    s_out = _rms(y, gs_ref[...])
    mix = (jnp.dot(a_ref[rows, :], wout_ref[:d_attn, :], preferred_element_type=F32)
           + jnp.dot(s_out.astype(BF16), wout_ref[d_attn:, :], preferred_element_type=F32))
    h = x_ref[rows, :] + mix
    h_ref[rows, :] = h
    hn = _rms(h, gffn_ref[...])
    for c in range(nch):
        hn_ref[pl.ds(sub * tm * nch + c, tm, stride=nch), :] = hn[:, c * LANES:(c + 1) * LANES]

    lg = jnp.dot(hn, wr_ref[...], preferred_element_type=F32, precision=lax.Precision.HIGHEST) + br_ref[...]
    lane = lax.broadcasted_iota(I32, (tm, LANES), 1)
    lane_f = lane.astype(F32)
    vals, hots = [], []
    idx_all = jnp.zeros((tm, LANES), F32)
    for kk in range(TOP_K):
        mx = jnp.max(lg, axis=-1, keepdims=True)
        ix = jnp.min(jnp.where(lg == mx, lane_f, float(LANES)), axis=-1, keepdims=True)
        hot = lane_f == ix
        vals.append(mx)
        hots.append(hot)
        idx_all = jnp.where(lane == kk, ix, idx_all)
        lg = jnp.where(hot, -3e38, lg)
    ex = [jnp.exp(v - vals[0]) for v in vals]
    den = ex[0] + ex[1] + ex[2] + ex[3]
    gate_all = jnp.zeros((tm, LANES), F32)
    for kk in range(TOP_K):
        gate_all = jnp.where(lane == kk, ex[kk] / den, gate_all)

    picked = jnp.zeros((tm, LANES), F32)
    for hot in hots:
        picked = jnp.where(hot, 1.0, picked)
    tri = (lax.broadcasted_iota(I32, (tm, tm), 1) < lax.broadcasted_iota(I32, (tm, tm), 0))
    before = jnp.dot(jnp.where(tri, 1.0, 0.0).astype(BF16), picked.astype(BF16), preferred_element_type=F32)
    before = before + run_ref[...]
    rank_all = jnp.zeros((tm, LANES), F32)
    for kk, hot in enumerate(hots):
        rk = jnp.sum(jnp.where(hot, before, 0.0), axis=-1, keepdims=True)
        rank_all = jnp.where(lane == kk, rk, rank_all)
    run_ref[...] = run_ref[...] + jnp.sum(picked, axis=0, keepdims=True)
    idx_ref[rows, :] = idx_all[:, :TOP_K].astype(I32)
    gate_ref[rows, :] = gate_all[:, :TOP_K]
    rank_ref[rows, :] = rank_all[:, :TOP_K].astype(I32)


def _mix(x2, a_out, yt4, u, ssm_d, w_glu, b_glu, ssm_norm_g, w_out, ln_ffn_g, w_router, b_router):
    t, d = x2.shape
    d_attn = a_out.shape[1]
    bsz, d_ssm, kc, _ = yt4.shape
    s_len = t // bsz
    n_exp = w_router.shape[1]
    tm = min(TOKEN_TILE, s_len)
    tiles = s_len // tm
    nch = d // LANES
    wr = jnp.zeros((d, LANES), F32).at[:, :n_exp].set(w_router.astype(F32))
    br = jnp.full((1, LANES), NEG_BIG, F32).at[0, :n_exp].set(b_router.astype(F32))
    row = lambda i: (i, 0)
    const = lambda i: (0, 0)
    vec = lambda a: a.reshape(1, -1).astype(F32)
    return pl.pallas_call(
        functools.partial(_mix_kernel, d_attn=d_attn),
        out_shape=(jax.ShapeDtypeStruct((t, d), F32), jax.ShapeDtypeStruct((t * nch, LANES), F32),
                   jax.ShapeDtypeStruct((t, TOP_K), I32), jax.ShapeDtypeStruct((t, TOP_K), F32),
                   jax.ShapeDtypeStruct((t, TOP_K), I32), jax.ShapeDtypeStruct((1, LANES), F32)),
        grid=(t // tm,),
        in_specs=[pl.BlockSpec((tm, d), row), pl.BlockSpec((tm, d_attn), row),
                  pl.BlockSpec((None, d_ssm, tm // LANES, LANES), lambda i: (i // tiles, 0, i % tiles, 0)),
                  pl.BlockSpec((tm, d_ssm), row), pl.BlockSpec((1, d_ssm), const),
                  pl.BlockSpec((d_ssm, d_ssm), const), pl.BlockSpec((1, d_ssm), const),
                  pl.BlockSpec((1, d_ssm), const), pl.BlockSpec((d_attn + d_ssm, d), const),
                  pl.BlockSpec((1, d), const), pl.BlockSpec((d, LANES), const), pl.BlockSpec((1, LANES), const)],
        out_specs=(pl.BlockSpec((tm, d), row), pl.BlockSpec((tm * nch, LANES), row),
                   pl.BlockSpec((tm, TOP_K), row), pl.BlockSpec((tm, TOP_K), row), pl.BlockSpec((tm, TOP_K), row),
                   pl.BlockSpec((1, LANES), const)),
        scratch_shapes=[pltpu.VMEM((1, LANES), F32)],
        compiler_params=pltpu.CompilerParams(dimension_semantics=("arbitrary",), vmem_limit_bytes=VMEM_LIMIT),
        name="mix",
    )(x2, a_out, yt4, u, vec(ssm_d), w_glu.astype(BF16), vec(b_glu), vec(ssm_norm_g), w_out.astype(BF16),
      vec(ln_ffn_g), wr, br)


IDX_RING = 4


def _moe_kernel(blk_e_ref, nused_ref, slot_hbm, hn_hbm, wgu_ref, bgu_ref, wd_ref, bd_ref, ytok_hbm,
                idx_smem, xbuf, ybuf, wperm, wdown, idx_sem, g_sem, s_sem, *,
                n_tok, n_blocks, tok_pad, rows, nch):
    i = pl.program_id(0)
    n_used = nused_ref[0]
    blk_rows = rows * nch
    d_ff = wdown.shape[0]
    grp = 2 * LANES

    def idx_copy(blk, src=None):
        src = jnp.minimum(blk, n_blocks - 1) if src is None else src
        slot = blk & (IDX_RING - 1)
        return pltpu.make_async_copy(slot_hbm.at[src], idx_smem.at[slot], idx_sem.at[slot])

    def half(par):
        return pl.ds(par * blk_rows, blk_rows)

    def gather_wait(par):
        pltpu.make_async_copy(hn_hbm.at[pl.ds(0, blk_rows)], xbuf.at[half(par)], g_sem.at[par]).wait()

    def scatter_wait(par):
        pltpu.make_async_copy(ybuf.at[half(par)], ytok_hbm.at[pl.ds(0, blk_rows)], s_sem.at[par]).wait()

    def issue_gather(blk, par):
        slot = blk & (IDX_RING - 1)
        for r in range(rows):
            src = pl.ds(pl.multiple_of(idx_smem[slot, 0, r], nch), nch)
            pltpu.make_async_copy(hn_hbm.at[src], xbuf.at[pl.ds(par * blk_rows + r * nch, nch)],
                                  g_sem.at[par]).start(priority=r % 2)

    def issue_scatter(blk, par):
        slot = blk & (IDX_RING - 1)
        for r in range(rows):
            dst = pl.ds(pl.multiple_of(idx_smem[slot, 1, r], nch), nch)
            pltpu.make_async_copy(ybuf.at[pl.ds(par * blk_rows + r * nch, nch)], ytok_hbm.at[dst],
                                  s_sem.at[par]).start(priority=r % 2)

    @pl.when(i == 0)
    def _():
        idx_copy(0).start()
        idx_copy(1).start()
        idx_copy(-1, src=n_blocks - 1).start()
        spare = (tok_pad - n_tok) * nch
        ybuf[...] = jnp.zeros_like(ybuf)
        fills = [pltpu.make_async_copy(ybuf.at[pl.ds(0, spare)],
                                       ytok_hbm.at[pl.ds((kk * tok_pad + n_tok) * nch, spare)], s_sem.at[0])
                 for kk in range(TOP_K)]
        for f in fills:
            f.start()
        for f in fills:
            f.wait()
        idx_copy(0).wait()
        idx_copy(-1, src=n_blocks - 1).wait()
        issue_gather(0, 0)

    def block(par):
        idx_copy(i + 1).wait()
        gather_wait(par)

        @pl.when(i > 0)
        def _():
            scatter_wait(par)

        expert = blk_e_ref[i]

        @pl.when(jnp.logical_or(i == 0, expert != blk_e_ref[jnp.maximum(i - 1, 0)]))
        def _():
            src = lax.broadcasted_iota(I32, (grp, grp), 0)
            col = lax.broadcasted_iota(I32, (grp, grp), 1)
            want = jnp.where(col < LANES, 2 * col, 2 * (col - LANES) + 1)
            perm = jnp.where(src == want, 1.0, 0.0).astype(BF16)
            for gi in range(wperm.shape[1] // grp):
                cols = slice(gi * grp, (gi + 1) * grp)
                wperm[:, cols] = jnp.dot(wgu_ref[:, cols].astype(BF16), perm,
                                         preferred_element_type=F32).astype(BF16)
            wdown[...] = wd_ref[...].astype(BF16)

        issue_gather(i + 1, 1 - par)
        issue_scatter(i - 1, 1 - par)
        x = jnp.concatenate([xbuf[pl.ds(par * blk_rows + c, rows, stride=nch), :] for c in range(nch)],
                            axis=-1).astype(BF16)
        gu = jnp.dot(x, wperm[...], preferred_element_type=F32) + bgu_ref[...]
        acts = []
        for gi in range(d_ff // LANES):
            gate = jnp.minimum(gu[:, gi * grp:gi * grp + LANES], SWIGLU_LIMIT)
            up = jnp.clip(gu[:, gi * grp + LANES:(gi + 1) * grp], -SWIGLU_LIMIT, SWIGLU_LIMIT)
            acts.append((gate * jax.nn.sigmoid(SWIGLU_ALPHA * gate) * (up + 1.0)).astype(BF16))
        y = jnp.dot(jnp.concatenate(acts, axis=-1), wdown[...], preferred_element_type=F32) + bd_ref[...]
        for c in range(nch):
            ybuf[pl.ds(par * blk_rows + c, rows, stride=nch), :] = y[:, c * LANES:(c + 1) * LANES]
        idx_copy(i + 2).start()

        @pl.when(i == n_used - 1)
        def _():
            issue_scatter(i, par)
            scatter_wait(1 - par)
            scatter_wait(par)
            gather_wait(1 - par)
            idx_copy(i + 2).wait()

    for par in range(2):
        pl.when(jnp.logical_and(i < n_used, (i & 1) == par))(functools.partial(block, par))


def _moe(hn2, slot_rows, blk_expert, n_used, wgu, bgu, wd, bd, n_tok, tok_pad):
    n_blocks, _, rows = slot_rows.shape
    n_exp, d, d_ff2 = wgu.shape
    d_ff = d_ff2 // 2
    nch = d // LANES
    wmap = lambda i, be, nu: (be[i], 0, 0)
    return pl.pallas_call(
        functools.partial(_moe_kernel, n_tok=n_tok, n_blocks=n_blocks, tok_pad=tok_pad, rows=rows, nch=nch),
        out_shape=jax.ShapeDtypeStruct((TOP_K * tok_pad * nch, LANES), F32),
        grid_spec=pltpu.PrefetchScalarGridSpec(
            num_scalar_prefetch=2,
            grid=(n_blocks,),
            in_specs=[pl.BlockSpec(memory_space=pl.ANY), pl.BlockSpec(memory_space=pl.ANY),
                      pl.BlockSpec((None, d, d_ff2), wmap), pl.BlockSpec((None, 1, d_ff2), wmap),
                      pl.BlockSpec((None, d_ff, d), wmap), pl.BlockSpec((None, 1, d), wmap)],
            out_specs=pl.BlockSpec(memory_space=pl.ANY),
            scratch_shapes=[pltpu.SMEM((IDX_RING, 2, rows), I32), pltpu.VMEM((2 * rows * nch, LANES), F32),
                            pltpu.VMEM((2 * rows * nch, LANES), F32), pltpu.VMEM((d, d_ff2), BF16),
                            pltpu.VMEM((d_ff, d), BF16), pltpu.SemaphoreType.DMA((IDX_RING,)),
                            pltpu.SemaphoreType.DMA((2,)), pltpu.SemaphoreType.DMA((2,))]),
        compiler_params=pltpu.CompilerParams(dimension_semantics=("arbitrary",), vmem_limit_bytes=VMEM_LIMIT),
        name="moe",
    )(blk_expert, n_used, slot_rows, hn2, wgu, bgu, wd, bd)


def _combine_kernel(h_ref, y_ref, gate_ref, g_ref, o_ref):
    tm, d = h_ref.shape
    nch = d // LANES
    gates = gate_ref[...]
    parts = []
    ss = None
    for c in range(nch):
        acc = h_ref[:, c * LANES:(c + 1) * LANES]
        for kk in range(TOP_K):
            acc = acc + gates[:, kk:kk + 1] * y_ref[kk, pl.ds(c, tm, stride=nch), :]
        parts.append(acc)
        sq = jnp.sum(acc * acc, axis=-1, keepdims=True)
        ss = sq if ss is None else ss + sq
    inv = lax.rsqrt(ss / (nch * LANES) + RMS_EPS)
    for c in range(nch):
        o_ref[:, c * LANES:(c + 1) * LANES] = parts[c] * inv * g_ref[:, c * LANES:(c + 1) * LANES]


def _combine(h, ytok3, gates, final_g):
    t, d = h.shape
    nch = d // LANES
    tm = min(TOKEN_TILE // 2, t)
    return pl.pallas_call(
        _combine_kernel,
        out_shape=jax.ShapeDtypeStruct((t, d), F32),
        grid=(t // tm,),
        in_specs=[pl.BlockSpec((tm, d), lambda i: (i, 0)),
                  pl.BlockSpec((TOP_K, tm * nch, LANES), lambda i: (0, i, 0)),
                  pl.BlockSpec((tm, TOP_K), lambda i: (i, 0)), pl.BlockSpec((1, d), lambda i: (0, 0))],
        out_specs=pl.BlockSpec((tm, d), lambda i: (i, 0)),
        compiler_params=pltpu.CompilerParams(dimension_semantics=("arbitrary",), vmem_limit_bytes=VMEM_LIMIT),
        name="combine",
    )(h, ytok3, gates, final_g.reshape(1, d).astype(F32))


def _moe_ffn(h, hn2, idx, gates, rank, counts, w_gate_up, b_gate_up, w_down, b_down, final_g):
    t, d = h.shape
    n_exp = w_gate_up.shape[0]
    d_ff = w_down.shape[1]
    rows = MOE_ROWS
    n_assign = t * TOP_K
    n_rows = n_assign + n_exp * rows
    n_blocks = n_rows // rows
    tok_pad = t + (2 * rows) // TOP_K

    cnt = counts[0, :n_exp].astype(I32)
    padded = (cnt + rows - 1) // rows * rows
    pad_ends = jnp.cumsum(padded)
    pad_starts = pad_ends - padded
    dest = pad_starts[idx] + rank
    dump = n_assign + jnp.arange(n_rows, dtype=I32) % (2 * rows)
    slot_rows = dump.at[dest.reshape(-1)].set(jnp.arange(n_assign, dtype=I32), unique_indices=True)
    blk_start = jnp.arange(n_blocks, dtype=I32) * rows
    blk_expert = jnp.minimum(jnp.sum((pad_ends[None, :] <= blk_start[:, None]).astype(I32), axis=1), n_exp - 1)
    n_used = (pad_ends[-1:] // rows).astype(I32)

    bgu = (b_gate_up.astype(F32).reshape(n_exp, d_ff // LANES, LANES, 2).transpose(0, 1, 3, 2)
           .reshape(n_exp, 1, 2 * d_ff))
    nch = d // LANES
    src_row = jnp.minimum(slot_rows >> 2, t - 1) * nch
    dst_row = ((slot_rows & (TOP_K - 1)) * tok_pad + (slot_rows >> 2)) * nch
    row_idx = jnp.stack([src_row.reshape(n_blocks, rows), dst_row.reshape(n_blocks, rows)], axis=1)
    ytok = _moe(hn2, row_idx, blk_expert, n_used, w_gate_up, bgu,
                w_down, b_down.reshape(n_exp, 1, d).astype(F32), t, tok_pad)
    return _combine(h, ytok.reshape(TOP_K, tok_pad * (d // LANES), LANES), gates, final_g)


def kernel(x, positions, ln_mix_g, w_in, lam_q1, lam_k1, lam_q2, lam_k2, diff_norm_g, ssm_lam_re, ssm_lam_im,
           ssm_b_re, ssm_b_im, ssm_c_re, ssm_c_im, ssm_d, ssm_log_dt, ssm_w_glu, ssm_b_glu, ssm_norm_g, w_out,
           ln_ffn_g, w_router, b_router, w_gate_up, b_gate_up, w_down, b_down, final_norm_g):
    bsz, s_len, d = x.shape
    depth = w_in.shape[0]
    assert depth == 1
    li = 0
    lambda_init = 0.8 - 0.6 * math.exp(-0.3 * li)
    d_attn = ATTN_HEADS * LANES
    d_ssm = w_in.shape[2] - 3 * d_attn
    t = bsz * s_len

    x2 = x.reshape(t, d)
    q, k, v, u, ut4 = _inproj(x2, positions.reshape(t, 1), ln_mix_g[li], w_in[li], d_attn, d_ssm, bsz)
    vt = v.reshape(bsz, s_len, d_attn).transpose(0, 2, 1)
    a_out = _attention(q.reshape(bsz, s_len, d_attn), k.reshape(bsz, s_len, d_attn), vt,
                       lam_q1[li], lam_k1[li], lam_q2[li], lam_k2[li], diff_norm_g[li], lambda_init)
    yt4 = _s5_core(ut4, ssm_lam_re[li], ssm_lam_im[li], ssm_b_re[li], ssm_b_im[li],
                   ssm_c_re[li], ssm_c_im[li], ssm_log_dt[li])
    h, hn2, idx, gates, rank, counts = _mix(
        x2, a_out.reshape(t, d_attn), yt4, u, ssm_d[li], ssm_w_glu[li], ssm_b_glu[li], ssm_norm_g[li],
        w_out[li], ln_ffn_g[li], w_router[li], b_router[li])
    out = _moe_ffn(h, hn2, idx, gates, rank, counts, w_gate_up[li], b_gate_up[li], w_down[li], b_down[li],
                   final_norm_g)
    return out.reshape(bsz, s_len, d)
```

```python
import functools
import math

import jax
import jax.numpy as jnp
from jax import lax
from jax.experimental import pallas as pl
from jax.experimental.pallas import tpu as pltpu

F32 = jnp.float32
BF16 = jnp.bfloat16
I32 = jnp.int32

RMS_EPS = 1e-6
ATTN_HEADS = 4
ROPE_DIM = 16
ROPE_THETA = 500000.0
SSM_GROUP = 16
SSM_STATE = 64
TOP_K = 4
SWIGLU_LIMIT = 7.0
SWIGLU_ALPHA = 1.702
LANES = 128
SUBLANES = 8
NEG_BIG = -1e30

TOKEN_TILE = 1024
MIX_SUBTILE = 512
ATTN_TILE = 512
ATTN_QUERY_SPLITS = 1
SHIFT_SLACK = 64.0
SSM_BATCH_PAD = SUBLANES
SCAN_BLOCK = 32
MOE_ROWS = 256
VMEM_LIMIT = 56 * 1024 * 1024


def _rms(x, g):
    return x * lax.rsqrt(jnp.mean(x * x, axis=-1, keepdims=True) + RMS_EPS) * g


def _dot_t(a, b, **kw):
    return lax.dot_general(a, b, (((1,), (1,)), ((), ())), preferred_element_type=F32, **kw)


def _inproj_kernel(x_ref, pos_ref, g_ref, w_ref, freq_ref, q_ref, k_ref, v_ref, u_ref, ut_ref, *, d_attn):
    x = x_ref[...]
    n = _rms(x, g_ref[...]).astype(BF16)
    proj = jnp.dot(n, w_ref[...], preferred_element_type=F32)
    ang = pos_ref[...].astype(F32) * freq_ref[...]
    lane = lax.broadcasted_iota(I32, (1, LANES), 1) & (LANES // 2 - 1)
    half = ROPE_DIM // 2
    first = lane < half
    second = jnp.logical_and(lane >= half, lane < ROPE_DIM)
    cos = jnp.cos(ang)
    sin = jnp.sin(ang)
    cm = jnp.where(lane < ROPE_DIM, cos, 1.0)
    sa = jnp.where(first, -sin, 0.0)
    sb = jnp.where(second, sin, 0.0)

    def rope(t):
        return t * cm + pltpu.roll(t, LANES - half, 1) * sa + pltpu.roll(t, half, 1) * sb

    scale = (LANES // 2) ** -0.5 * math.log2(math.e)
    for h in range(d_attn // LANES):
        sl = slice(h * LANES, (h + 1) * LANES)
        q_ref[:, sl] = (rope(proj[:, sl]) * scale).astype(BF16)
        k_ref[:, sl] = rope(proj[:, d_attn + h * LANES:d_attn + (h + 1) * LANES]).astype(BF16)
    v_ref[...] = proj[:, 2 * d_attn:3 * d_attn].astype(BF16)
    u = proj[:, 3 * d_attn:]
    u_ref[...] = u.astype(BF16)
    ut = u.T
    for c in range(ut_ref.shape[1]):
        ut_ref[:, c, :] = ut[:, c * LANES:(c + 1) * LANES]


def _inproj(x2, pos2, g, w_in, d_attn, d_ssm, bsz):
    t, d = x2.shape
    s_len = t // bsz
    tm = min(TOKEN_TILE, s_len)
    tiles = s_len // tm
    cpt = tm // LANES
    assert s_len % tm == 0 and cpt % SUBLANES == 0
    n_in = w_in.shape[1]
    inv_freq = ROPE_THETA ** (-jnp.arange(0, ROPE_DIM, 2, dtype=F32) / ROPE_DIM)
    lane = jnp.arange(LANES) % (LANES // 2)
    freq = jnp.where(lane < ROPE_DIM, inv_freq[lane % (ROPE_DIM // 2)], 0.0).reshape(1, LANES)
    row = lambda i: (i, 0)
    const = lambda i: (0, 0)
    return pl.pallas_call(
        functools.partial(_inproj_kernel, d_attn=d_attn),
        out_shape=(jax.ShapeDtypeStruct((t, d_attn), BF16),) * 3 + (
            jax.ShapeDtypeStruct((t, d_ssm), BF16), jax.ShapeDtypeStruct((bsz, d_ssm, s_len // LANES, LANES), F32)),
        grid=(t // tm,),
        in_specs=[pl.BlockSpec((tm, d), row), pl.BlockSpec((tm, 1), row), pl.BlockSpec((1, d), const),
                  pl.BlockSpec((d, n_in), const), pl.BlockSpec((1, LANES), const)],
        out_specs=(pl.BlockSpec((tm, d_attn), row),) * 3 + (
            pl.BlockSpec((tm, d_ssm), row),
            pl.BlockSpec((None, d_ssm, cpt, LANES), lambda i: (i // tiles, 0, i % tiles, 0))),
        compiler_params=pltpu.CompilerParams(dimension_semantics=("arbitrary",), vmem_limit_bytes=VMEM_LIMIT),
        name="inproj",
    )(x2, pos2, g.reshape(1, d), w_in.astype(BF16), freq)


def _attn_kernel(q_ref, k_ref, vt_ref, lq1_ref, lk1_ref, lq2_ref, lk2_ref, g_ref, o_ref, acc1, acc2, *,
                 tile, lambda_init):
    qi = pl.program_id(2)
    q = q_ref[...]
    lane = lax.broadcasted_iota(I32, (1, LANES), 1)
    zero = jnp.zeros_like(q)
    q_maps = (jnp.where(lane < LANES // 2, q, zero), jnp.where(lane >= LANES // 2, q, zero))
    accs = (acc1, acc2)
    acc1[...] = jnp.zeros_like(acc1)
    acc2[...] = jnp.zeros_like(acc2)
    half = tile // ATTN_QUERY_SPLITS
    units = [(mi, h) for mi in range(2) for h in range(ATTN_QUERY_SPLITS)]
    m0 = jnp.full((1, half), NEG_BIG, F32)
    l0 = jnp.zeros((1, half), F32)

    def scores(j, unit, masked):
        mi, h = unit
        k = k_ref[pl.ds(pl.multiple_of(j * tile, tile), tile), :]
        s = _dot_t(k, q_maps[mi][h * half:(h + 1) * half])
        if masked:
            keep = (lax.broadcasted_iota(I32, (tile, half), 0)
                    <= lax.broadcasted_iota(I32, (tile, half), 1) + h * half)
            s = jnp.where(keep, s, NEG_BIG)
        return s

    def values_t(j):
        return vt_ref[:, pl.ds(pl.multiple_of(j * tile, tile), tile)]

    def rescaling_step(j, carry, masked):
        vt = values_t(j)
        out = []
        for ui, (mi, h) in enumerate(units):
            m, l = carry[2 * ui], carry[2 * ui + 1]
            cols = slice(h * half, (h + 1) * half)
            s = scores(j, (mi, h), masked)
            m_new = jnp.maximum(m, jnp.max(s, axis=0, keepdims=True))
            alpha = jnp.exp2(m - m_new)
            p = jnp.exp2(s - m_new)
            l_new = alpha * l + jnp.sum(p, axis=0, keepdims=True)
            accs[mi][:, cols] = (accs[mi][:, cols] * alpha
                                 + jnp.dot(vt, p.astype(BF16), preferred_element_type=F32))
            out += [m_new, l_new]
        return tuple(out)

    def step(blocks, carry):
        pend = [None] * (2 * len(units))
        worst = None
        ready = []
        for blk in list(blocks) + [None]:
            fresh = [] if blk is None else [(scores(blk[0], unit, blk[1]), values_t(blk[0])) for unit in units]
            for ui, (s, vt) in enumerate(ready):
                m = carry[2 * ui]
                p = jnp.exp2(s - m)
                over = jnp.max(s, axis=0, keepdims=True) - m
                worst = over if worst is None else jnp.maximum(worst, over)
                new = (jnp.dot(vt, p.astype(BF16), preferred_element_type=F32), jnp.sum(p, axis=0, keepdims=True))
                for t in range(2):
                    pend[2 * ui + t] = new[t] if pend[2 * ui + t] is None else pend[2 * ui + t] + new[t]
            ready = fresh

        def keep_shift():
            out = []
            for ui, (mi, h) in enumerate(units):
                cols = slice(h * half, (h + 1) * half)
                accs[mi][:, cols] = accs[mi][:, cols] + pend[2 * ui]
                out += [carry[2 * ui], carry[2 * ui + 1] + pend[2 * ui + 1]]
            return tuple(out)

        def rescale():
            c = carry
            for j, masked in blocks:
                c = rescaling_step(j, c, masked)
            return c

        return lax.cond(jnp.max(worst) <= SHIFT_SLACK, keep_shift, rescale)

    init = (m0, l0) * len(units)
    carry = lax.cond(qi == 0, lambda: rescaling_step(0, init, True), lambda: rescaling_step(0, init, False))
    pairs = jnp.maximum(qi - 1, 0) // 2
    carry = lax.fori_loop(0, pairs, lambda t, c: step([(1 + 2 * t, False), (2 + 2 * t, False)], c), carry)
    odd = 1 + 2 * pairs
    carry = lax.cond(odd < qi, lambda: step([(odd, False)], carry), lambda: carry)
    carry = lax.cond(qi > 0, lambda: step([(qi, True)], carry), lambda: carry)
    ns = ATTN_QUERY_SPLITS
    l1 = jnp.concatenate([carry[2 * ui + 1] for ui in range(ns)], axis=-1)
    l2 = jnp.concatenate([carry[2 * ui + 1] for ui in range(ns, 2 * ns)], axis=-1)

    lam = (jnp.exp(jnp.sum(lq1_ref[...] * lk1_ref[...], axis=-1, keepdims=True))
           - jnp.exp(jnp.sum(lq2_ref[...] * lk2_ref[...], axis=-1, keepdims=True)) + lambda_init)
    ot = acc1[...] / l1 - lam * (acc2[...] / l2)
    o = _rms(ot.T, g_ref[...]) * (1.0 - lambda_init)
    o_ref[...] = o.astype(o_ref.dtype)


def _attention(q, k, vt, lq1, lk1, lq2, lk2, norm_g, lambda_init):
    b, s, d_attn = q.shape
    heads = d_attn // LANES
    tile = min(ATTN_TILE, s)
    dk = lq1.shape[-1]
    vec = lambda a: a.reshape(1, -1).astype(F32)
    const = lambda bi, h, qi: (0, 0)
    return pl.pallas_call(
        functools.partial(_attn_kernel, tile=tile, lambda_init=lambda_init),
        out_shape=jax.ShapeDtypeStruct((b, s, d_attn), BF16),
        grid=(b, heads, s // tile),
        in_specs=[pl.BlockSpec((None, tile, LANES), lambda bi, h, qi: (bi, qi, h)),
                  pl.BlockSpec((None, s, LANES), lambda bi, h, qi: (bi, 0, h)),
                  pl.BlockSpec((None, LANES, s), lambda bi, h, qi: (bi, h, 0)),
                  pl.BlockSpec((1, dk), const), pl.BlockSpec((1, dk), const),
                  pl.BlockSpec((1, dk), const), pl.BlockSpec((1, dk), const),
                  pl.BlockSpec((1, LANES), const)],
        out_specs=pl.BlockSpec((None, tile, LANES), lambda bi, h, qi: (bi, qi, h)),
        scratch_shapes=[pltpu.VMEM((LANES, tile), F32), pltpu.VMEM((LANES, tile), F32)],
        compiler_params=pltpu.CompilerParams(dimension_semantics=("arbitrary",) * 3,
                                             vmem_limit_bytes=VMEM_LIMIT),
        name="attn",
    )(q, k, vt, vec(lq1), vec(lk1), vec(lq2), vec(lk2), vec(norm_g))


def _ssm_consts(lre, lim, ldt):
    dt = jnp.exp(ldt)
    return lre * dt, lim * dt


def _cpow(e, ldr, ldi):
    mag = jnp.exp(e * ldr)
    ang = e * ldi
    return mag * jnp.cos(ang), mag * jnp.sin(ang)


def _bbar(lre, lim, ldr, ldi, bre, bim):
    mag = jnp.exp(ldr)
    nr = mag * jnp.cos(ldi) - 1.0
    ni = mag * jnp.sin(ldi)
    den = lre * lre + lim * lim
    cr = (nr * lre + ni * lim) / den
    ci = (ni * lre - nr * lim) / den
    return cr * bre - ci * bim, cr * bim + ci * bre


def _crows(pr, pi, xr, xi):
    nc = xr.shape[0]
    rr = jnp.concatenate([pr * xr[c:c + 1] - pi * xi[c:c + 1] for c in range(nc)], axis=0)
    ri = jnp.concatenate([pr * xi[c:c + 1] + pi * xr[c:c + 1] for c in range(nc)], axis=0)
    return rr, ri


def _chunk_rows(ut_ref, gi, uc3):
    bsz, nc2, kc, _ = ut_ref.shape
    nc = nc2 // 2
    for b in range(bsz):
        for c in range(nc):
            uc3[:, b, c * LANES:(c + 1) * LANES] = ut_ref[b, gi * nc + c]
    return uc3[...].reshape(kc * uc3.shape[1], nc * LANES).astype(BF16)


def _s5_local_kernel(ut_ref, lre_ref, lim_ref, ldt_ref, bre_ref, bim_ref, zr_ref, zi_ref, uc3, *, chunk):
    @pl.when(pl.program_id(0) == 0)
    def _():
        uc3[...] = jnp.zeros_like(uc3)

    lane = lax.broadcasted_iota(I32, (1, LANES), 1)
    e = (chunk - 1.0) - lax.broadcasted_iota(I32, (chunk, 1), 0).astype(F32)
    zr = None
    for gi in range(2):
        lre, lim = lre_ref[gi], lim_ref[gi]
        ldr, ldi = _ssm_consts(lre, lim, ldt_ref[gi])
        bbr, bbi = _bbar(lre, lim, ldr, ldi, bre_ref[gi], bim_ref[gi])
        pr, pi = _cpow(e, ldr, ldi)
        wr, wi = _crows(pr, pi, bbr, bbi)
        mine = (lane < LANES // 2) if gi == 0 else (lane >= LANES // 2)
        u = _chunk_rows(ut_ref, gi, uc3)
        pr_ = jnp.dot(u, jnp.where(mine, wr, 0.0).astype(BF16), preferred_element_type=F32)
        pi_ = jnp.dot(u, jnp.where(mine, wi, 0.0).astype(BF16), preferred_element_type=F32)
        zr, zi = (pr_, pi_) if zr is None else (zr + pr_, zi + pi_)
    zr_ref[...] = zr
    zi_ref[...] = zi


def _s5_scan_kernel(zr_ref, zi_ref, lre_ref, lim_ref, ldt_ref, xr_ref, xi_ref, sr, si, ar, ai, *, chunk, nblk):
    @pl.when(pl.program_id(0) == 0)
    def _():
        ldr, ldi = _ssm_consts(lre_ref[...], lim_ref[...], ldt_ref[...])
        mag = jnp.exp(chunk * ldr)
        ar[...] = jnp.broadcast_to(mag * jnp.cos(chunk * ldi), ar.shape)
        ai[...] = jnp.broadcast_to(mag * jnp.sin(chunk * ldi), ai.shape)
        sr[...] = jnp.zeros_like(sr)
        si[...] = jnp.zeros_like(si)

    a_r = ar[...]
    a_i = ai[...]

    def body(kk, carry):
        xr, xi = carry
        rows = pl.ds(pl.multiple_of(kk * SSM_BATCH_PAD, SSM_BATCH_PAD), SSM_BATCH_PAD)
        xr_ref[rows, :] = xr
        xi_ref[rows, :] = xi
        return (a_r * xr - a_i * xi + zr_ref[rows, :], a_r * xi + a_i * xr + zi_ref[rows, :])

    xr, xi = lax.fori_loop(0, nblk, body, (sr[...], si[...]))
    sr[...] = xr
    si[...] = xi


def _s5_out_kernel(ut_ref, xr_ref, xi_ref, lre_ref, lim_ref, ldt_ref, bre_ref, bim_ref, cre_ref, cim_ref,
                   yt_ref, uc3, y3, toep, kern, *, chunk):
    @pl.when(pl.program_id(0) == 0)
    def _():
        uc3[...] = jnp.zeros_like(uc3)

    bsz, nc2, kc, _ = ut_ref.shape
    nc = nc2 // 2
    lane = lax.broadcasted_iota(I32, (1, LANES), 1)
    low = lane < LANES // 2
    step = lax.broadcasted_iota(I32, (chunk, 1), 0).astype(F32)
    causal = lax.broadcasted_iota(I32, (chunk, chunk), 1) >= lax.broadcasted_iota(I32, (chunk, chunk), 0)
    xr = xr_ref[...].astype(BF16)
    xi = xi_ref[...].astype(BF16)
    for gi in range(2):
        lre, lim = lre_ref[gi], lim_ref[gi]
        ldr, ldi = _ssm_consts(lre, lim, ldt_ref[gi])
        bbr, bbi = _bbar(lre, lim, ldr, ldi, bre_ref[gi], bim_ref[gi])
        cre, cim = cre_ref[gi], cim_ref[gi]
        pr, pi = _cpow(step, ldr, ldi)
        gr, gim = _crows(pr, pi, cre, cim)
        kmat = _dot_t(jnp.where(low, bbr, -bbi), jnp.where(low, gr, gim), precision=lax.Precision.HIGHEST)
        for c1 in range(nc):
            kern[c1] = kmat[c1:c1 + 1]

        def toeplitz_rows(c1, _):
            rows = pl.ds(pl.multiple_of(c1 * chunk, chunk), chunk)
            for c in range(nc):
                resp = jnp.broadcast_to(kern[c1, :, c * chunk:(c + 1) * chunk], (chunk, chunk))
                blk = pltpu.roll(resp, 0, 1, stride=1, stride_axis=0)
                toep[rows, c * chunk:(c + 1) * chunk] = jnp.where(causal, blk, 0.0).astype(BF16)
            return 0

        lax.fori_loop(0, nc, toeplitz_rows, 0)
        qr, qi = _cpow(step + 1.0, ldr, ldi)
        vr, vi = _crows(qr, qi, cre, cim)
        mine = low if gi == 0 else jnp.logical_not(low)
        y = jnp.dot(_chunk_rows(ut_ref, gi, uc3), toep[...], preferred_element_type=F32)
        y = y + _dot_t(xr, jnp.where(mine, vr, 0.0).astype(BF16))
        y = y - _dot_t(xi, jnp.where(mine, vi, 0.0).astype(BF16))
        y3[...] = y.reshape(y3.shape)
        for b in range(bsz):
            for c in range(nc):
                yt_ref[b, gi * nc + c] = y3[:, b, c * LANES:(c + 1) * LANES]


def _s5_core(ut4, lam_re, lam_im, b_re, b_im, c_re, c_im, log_dt):
    bsz, _, kc, chunk = ut4.shape
    g, n = lam_re.shape
    nc = SSM_GROUP
    bp = SSM_BATCH_PAD
    rows = kc * bp
    r = nc * chunk
    assert 2 * n == LANES and chunk == LANES and g % 2 == 0 and bsz <= bp and kc % min(SCAN_BLOCK, kc) == 0

    dup = lambda a: jnp.concatenate([a, a], axis=-1).astype(F32)
    lre2 = dup(lam_re)[:, None, :]
    lim2 = dup(lam_im)[:, None, :]
    ldt2 = jnp.broadcast_to(log_dt.astype(F32)[:, None, None], (g, 1, LANES))
    bre2 = dup(b_re.transpose(0, 2, 1))
    bim2 = dup(b_im.transpose(0, 2, 1))
    cre2 = dup(c_re)
    cim2 = dup(c_im)

    pair3 = lambda p: (p, 0, 0)
    vec_spec = pl.BlockSpec((2, 1, LANES), pair3)
    coef_spec = pl.BlockSpec((2, nc, LANES), pair3)
    u_spec = pl.BlockSpec((bsz, 2 * nc, kc, chunk), lambda p: (0, p, 0, 0))
    z_spec = pl.BlockSpec((rows, LANES), lambda p: (0, p))
    params = pltpu.CompilerParams(dimension_semantics=("arbitrary",), vmem_limit_bytes=VMEM_LIMIT)
    rows3 = pltpu.VMEM((kc, bp, r), F32)

    zr, zi = pl.pallas_call(
        functools.partial(_s5_local_kernel, chunk=chunk),
        out_shape=(jax.ShapeDtypeStruct((rows, g * n), F32),) * 2,
        grid=(g // 2,),
        in_specs=[u_spec, vec_spec, vec_spec, vec_spec, coef_spec, coef_spec],
        out_specs=(z_spec, z_spec),
        scratch_shapes=[rows3],
        compiler_params=params, name="s5_local",
    )(ut4, lre2, lim2, ldt2, bre2, bim2)

    nblk = min(SCAN_BLOCK, kc)
    flat = lambda a: a.astype(F32).reshape(1, g * n)
    ldt_flat = jnp.broadcast_to(log_dt.astype(F32)[:, None], (g, n)).reshape(1, g * n)
    blk = pl.BlockSpec((nblk * bp, g * n), lambda i: (i, 0))
    cst = pl.BlockSpec((1, g * n), lambda i: (0, 0))
    xr, xi = pl.pallas_call(
        functools.partial(_s5_scan_kernel, chunk=chunk, nblk=nblk),
        out_shape=(jax.ShapeDtypeStruct((rows, g * n), F32),) * 2,
        grid=(kc // nblk,),
        in_specs=[blk, blk, cst, cst, cst],
        out_specs=(blk, blk),
        scratch_shapes=[pltpu.VMEM((bp, g * n), F32)] * 4,
        compiler_params=params, name="s5_scan",
    )(zr, zi, flat(lam_re), flat(lam_im), ldt_flat)

    return pl.pallas_call(
        functools.partial(_s5_out_kernel, chunk=chunk),
        out_shape=jax.ShapeDtypeStruct(ut4.shape, F32),
        grid=(g // 2,),
        in_specs=[u_spec, z_spec, z_spec, vec_spec, vec_spec, vec_spec, coef_spec, coef_spec, coef_spec, coef_spec],
        out_specs=u_spec,
        scratch_shapes=[rows3, rows3, pltpu.VMEM((r, r), BF16), pltpu.VMEM((nc, 1, r), F32)],
        compiler_params=params, name="s5_out",
    )(ut4, xr, xi, lre2, lim2, ldt2, bre2, bim2, cre2, cim2)


def _mix_kernel(x_ref, a_ref, y_ref, u_ref, d_ref, wglu_ref, bglu_ref, gs_ref, wout_ref, gffn_ref,
                wr_ref, br_ref, h_ref, hn_ref, idx_ref, gate_ref, rank_ref, cnt_ref, run_ref, *, d_attn):
    i = pl.program_id(0)

    @pl.when(i == 0)
    def _():
        run_ref[...] = jnp.zeros_like(run_ref)

    tm = min(MIX_SUBTILE, x_ref.shape[0])
    nch = x_ref.shape[1] // LANES
    for sub in range(x_ref.shape[0] // tm):
        _mix_subtile(sub, tm, nch, d_attn, x_ref, a_ref, y_ref, u_ref, d_ref, wglu_ref, bglu_ref, gs_ref,
                     wout_ref, gffn_ref, wr_ref, br_ref, h_ref, hn_ref, idx_ref, gate_ref, rank_ref, run_ref)
    cnt_ref[...] = run_ref[...]


def _mix_subtile(sub, tm, nch, d_attn, x_ref, a_ref, y_ref, u_ref, d_ref, wglu_ref, bglu_ref, gs_ref, wout_ref,
                 gffn_ref, wr_ref, br_ref, h_ref, hn_ref, idx_ref, gate_ref, rank_ref, run_ref):
    rows = pl.ds(sub * tm, tm)
    cps = tm // LANES
    yt = jnp.concatenate([y_ref[:, sub * cps + c, :] for c in range(cps)], axis=-1)
    y = yt.T + d_ref[...] * u_ref[rows, :].astype(F32)
    y = 0.5 * y * (1.0 + lax.erf(y * (0.5 ** 0.5)))
    z = jnp.dot(y.astype(BF16), wglu_ref[...], preferred_element_type=F32) + bglu_ref[...]
    y = y * jax.nn.sigmoid(z)
    s_out = _rms(y, gs_ref[...])
    mix = (jnp.dot(a_ref[rows, :], wout_ref[:d_attn, :], preferred_element_type=F32)
           + jnp.dot(s_out.astype(BF16), wout_ref[d_attn:, :], preferred_element_type=F32))
    h = x_ref[rows, :] + mix
    h_ref[rows, :] = h
    hn = _rms(h, gffn_ref[...])
    for c in range(nch):
        hn_ref[pl.ds(sub * tm * nch + c, tm, stride=nch), :] = hn[:, c * LANES:(c + 1) * LANES]

    lg = jnp.dot(hn, wr_ref[...], preferred_element_type=F32, precision=lax.Precision.HIGHEST) + br_ref[...]
    lane = lax.broadcasted_iota(I32, (tm, LANES), 1)
    lane_f = lane.astype(F32)
    vals, hots = [], []
    idx_all = jnp.zeros((tm, LANES), F32)
    for kk in range(TOP_K):
        mx = jnp.max(lg, axis=-1, keepdims=True)
        ix = jnp.min(jnp.where(lg == mx, lane_f, float(LANES)), axis=-1, keepdims=True)
        hot = lane_f == ix
        vals.append(mx)
        hots.append(hot)
        idx_all = jnp.where(lane == kk, ix, idx_all)
        lg = jnp.where(hot, -3e38, lg)
    ex = [jnp.exp(v - vals[0]) for v in vals]
    den = ex[0] + ex[1] + ex[2] + ex[3]
    gate_all = jnp.zeros((tm, LANES), F32)
    for kk in range(TOP_K):
        gate_all = jnp.where(lane == kk, ex[kk] / den, gate_all)

    picked = jnp.zeros((tm, LANES), F32)
    for hot in hots:
        picked = jnp.where(hot, 1.0, picked)
    tri = (lax.broadcasted_iota(I32, (tm, tm), 1) < lax.broadcasted_iota(I32, (tm, tm), 0))
    before = jnp.dot(jnp.where(tri, 1.0, 0.0).astype(BF16), picked.astype(BF16), preferred_element_type=F32)
    before = before + run_ref[...]
    rank_all = jnp.zeros((tm, LANES), F32)
    for kk, hot in enumerate(hots):
        rk = jnp.sum(jnp.where(hot, before, 0.0), axis=-1, keepdims=True)
        rank_all = jnp.where(lane == kk, rk, rank_all)
    run_ref[...] = run_ref[...] + jnp.sum(picked, axis=0, keepdims=True)
    idx_ref[rows, :] = idx_all[:, :TOP_K].astype(I32)
    gate_ref[rows, :] = gate_all[:, :TOP_K]
    rank_ref[rows, :] = rank_all[:, :TOP_K].astype(I32)


def _mix(x2, a_out, yt4, u, ssm_d, w_glu, b_glu, ssm_norm_g, w_out, ln_ffn_g, w_router, b_router):
    t, d = x2.shape
    d_attn = a_out.shape[1]
    bsz, d_ssm, kc, _ = yt4.shape
    s_len = t // bsz
    n_exp = w_router.shape[1]
    tm = min(TOKEN_TILE, s_len)
    tiles = s_len // tm
    nch = d // LANES
    wr = jnp.zeros((d, LANES), F32).at[:, :n_exp].set(w_router.astype(F32))
    br = jnp.full((1, LANES), NEG_BIG, F32).at[0, :n_exp].set(b_router.astype(F32))
    row = lambda i: (i, 0)
    const = lambda i: (0, 0)
    vec = lambda a: a.reshape(1, -1).astype(F32)
    return pl.pallas_call(
        functools.partial(_mix_kernel, d_attn=d_attn),
        out_shape=(jax.ShapeDtypeStruct((t, d), F32), jax.ShapeDtypeStruct((t * nch, LANES), F32),
                   jax.ShapeDtypeStruct((t, TOP_K), I32), jax.ShapeDtypeStruct((t, TOP_K), F32),
                   jax.ShapeDtypeStruct((t, TOP_K), I32), jax.ShapeDtypeStruct((1, LANES), F32)),
        grid=(t // tm,),
        in_specs=[pl.BlockSpec((tm, d), row), pl.BlockSpec((tm, d_attn), row),
                  pl.BlockSpec((None, d_ssm, tm // LANES, LANES), lambda i: (i // tiles, 0, i % tiles, 0)),
                  pl.BlockSpec((tm, d_ssm), row), pl.BlockSpec((1, d_ssm), const),
                  pl.BlockSpec((d_ssm, d_ssm), const), pl.BlockSpec((1, d_ssm), const),
                  pl.BlockSpec((1, d_ssm), const), pl.BlockSpec((d_attn + d_ssm, d), const),
                  pl.BlockSpec((1, d), const), pl.BlockSpec((d, LANES), const), pl.BlockSpec((1, LANES), const)],
        out_specs=(pl.BlockSpec((tm, d), row), pl.BlockSpec((tm * nch, LANES), row),
                   pl.BlockSpec((tm, TOP_K), row), pl.BlockSpec((tm, TOP_K), row), pl.BlockSpec((tm, TOP_K), row),
                   pl.BlockSpec((1, LANES), const)),
        scratch_shapes=[pltpu.VMEM((1, LANES), F32)],
        compiler_params=pltpu.CompilerParams(dimension_semantics=("arbitrary",), vmem_limit_bytes=VMEM_LIMIT),
        name="mix",
    )(x2, a_out, yt4, u, vec(ssm_d), w_glu.astype(BF16), vec(b_glu), vec(ssm_norm_g), w_out.astype(BF16),
      vec(ln_ffn_g), wr, br)


IDX_RING = 4


def _moe_kernel(blk_e_ref, nused_ref, slot_hbm, hn_hbm, wgu_ref, bgu_ref, wd_ref, bd_ref, ytok_hbm,
                idx_smem, xbuf, ybuf, wperm, wdown, idx_sem, g_sem, s_sem, *,
                n_tok, n_blocks, tok_pad, rows, nch):
    i = pl.program_id(0)
    n_used = nused_ref[0]
    blk_rows = rows * nch
    d_ff = wdown.shape[0]
    grp = 2 * LANES

    def idx_copy(blk, src=None):
        src = jnp.minimum(blk, n_blocks - 1) if src is None else src
        slot = blk & (IDX_RING - 1)
        return pltpu.make_async_copy(slot_hbm.at[src], idx_smem.at[slot], idx_sem.at[slot])

    def half(par):
        return pl.ds(par * blk_rows, blk_rows)

    def gather_wait(par):
        pltpu.make_async_copy(hn_hbm.at[pl.ds(0, blk_rows)], xbuf.at[half(par)], g_sem.at[par]).wait()

    def scatter_wait(par):
        pltpu.make_async_copy(ybuf.at[half(par)], ytok_hbm.at[pl.ds(0, blk_rows)], s_sem.at[par]).wait()

    def issue_gather(blk, par):
        slot = blk & (IDX_RING - 1)
        for r in range(rows):
            src = pl.ds(pl.multiple_of(idx_smem[slot, 0, r], nch), nch)
            pltpu.make_async_copy(hn_hbm.at[src], xbuf.at[pl.ds(par * blk_rows + r * nch, nch)],
                                  g_sem.at[par]).start(priority=r % 2)

    def issue_scatter(blk, par):
        slot = blk & (IDX_RING - 1)
        for r in range(rows):
            dst = pl.ds(pl.multiple_of(idx_smem[slot, 1, r], nch), nch)
            pltpu.make_async_copy(ybuf.at[pl.ds(par * blk_rows + r * nch, nch)], ytok_hbm.at[dst],
                                  s_sem.at[par]).start(priority=r % 2)

    @pl.when(i == 0)
    def _():
        idx_copy(0).start()
        idx_copy(1).start()
        idx_copy(-1, src=n_blocks - 1).start()
        spare = (tok_pad - n_tok) * nch
        ybuf[...] = jnp.zeros_like(ybuf)
        fills = [pltpu.make_async_copy(ybuf.at[pl.ds(0, spare)],
                                       ytok_hbm.at[pl.ds((kk * tok_pad + n_tok) * nch, spare)], s_sem.at[0])
                 for kk in range(TOP_K)]
        for f in fills:
            f.start()
        for f in fills:
            f.wait()
        idx_copy(0).wait()
        idx_copy(-1, src=n_blocks - 1).wait()
        issue_gather(0, 0)

    def block(par):
        idx_copy(i + 1).wait()
        gather_wait(par)

        @pl.when(i > 0)
        def _():
            scatter_wait(par)

        expert = blk_e_ref[i]

        @pl.when(jnp.logical_or(i == 0, expert != blk_e_ref[jnp.maximum(i - 1, 0)]))
        def _():
            src = lax.broadcasted_iota(I32, (grp, grp), 0)
            col = lax.broadcasted_iota(I32, (grp, grp), 1)
            want = jnp.where(col < LANES, 2 * col, 2 * (col - LANES) + 1)
            perm = jnp.where(src == want, 1.0, 0.0).astype(BF16)
            for gi in range(wperm.shape[1] // grp):
                cols = slice(gi * grp, (gi + 1) * grp)
                wperm[:, cols] = jnp.dot(wgu_ref[:, cols].astype(BF16), perm,
                                         preferred_element_type=F32).astype(BF16)
            wdown[...] = wd_ref[...].astype(BF16)

        issue_gather(i + 1, 1 - par)
        issue_scatter(i - 1, 1 - par)
        x = jnp.concatenate([xbuf[pl.ds(par * blk_rows + c, rows, stride=nch), :] for c in range(nch)],
                            axis=-1).astype(BF16)
        gu = jnp.dot(x, wperm[...], preferred_element_type=F32) + bgu_ref[...]
        acts = []
        for gi in range(d_ff // LANES):
            gate = jnp.minimum(gu[:, gi * grp:gi * grp + LANES], SWIGLU_LIMIT)
            up = jnp.clip(gu[:, gi * grp + LANES:(gi + 1) * grp], -SWIGLU_LIMIT, SWIGLU_LIMIT)
            acts.append((gate * jax.nn.sigmoid(SWIGLU_ALPHA * gate) * (up + 1.0)).astype(BF16))
        y = jnp.dot(jnp.concatenate(acts, axis=-1), wdown[...], preferred_element_type=F32) + bd_ref[...]
        for c in range(nch):
            ybuf[pl.ds(par * blk_rows + c, rows, stride=nch), :] = y[:, c * LANES:(c + 1) * LANES]
        idx_copy(i + 2).start()

        @pl.when(i == n_used - 1)
        def _():
            issue_scatter(i, par)
            scatter_wait(1 - par)
            scatter_wait(par)
            gather_wait(1 - par)
            idx_copy(i + 2).wait()

    for par in range(2):
        pl.when(jnp.logical_and(i < n_used, (i & 1) == par))(functools.partial(block, par))


def _moe(hn2, slot_rows, blk_expert, n_used, wgu, bgu, wd, bd, n_tok, tok_pad):
    n_blocks, _, rows = slot_rows.shape
    n_exp, d, d_ff2 = wgu.shape
    d_ff = d_ff2 // 2
    nch = d // LANES
    wmap = lambda i, be, nu: (be[i], 0, 0)
    return pl.pallas_call(
        functools.partial(_moe_kernel, n_tok=n_tok, n_blocks=n_blocks, tok_pad=tok_pad, rows=rows, nch=nch),
        out_shape=jax.ShapeDtypeStruct((TOP_K * tok_pad * nch, LANES), F32),
        grid_spec=pltpu.PrefetchScalarGridSpec(
            num_scalar_prefetch=2,
            grid=(n_blocks,),
            in_specs=[pl.BlockSpec(memory_space=pl.ANY), pl.BlockSpec(memory_space=pl.ANY),
                      pl.BlockSpec((None, d, d_ff2), wmap), pl.BlockSpec((None, 1, d_ff2), wmap),
                      pl.BlockSpec((None, d_ff, d), wmap), pl.BlockSpec((None, 1, d), wmap)],
            out_specs=pl.BlockSpec(memory_space=pl.ANY),
            scratch_shapes=[pltpu.SMEM((IDX_RING, 2, rows), I32), pltpu.VMEM((2 * rows * nch, LANES), F32),
                            pltpu.VMEM((2 * rows * nch, LANES), F32), pltpu.VMEM((d, d_ff2), BF16),
                            pltpu.VMEM((d_ff, d), BF16), pltpu.SemaphoreType.DMA((IDX_RING,)),
                            pltpu.SemaphoreType.DMA((2,)), pltpu.SemaphoreType.DMA((2,))]),
        compiler_params=pltpu.CompilerParams(dimension_semantics=("arbitrary",), vmem_limit_bytes=VMEM_LIMIT),
        name="moe",
    )(blk_expert, n_used, slot_rows, hn2, wgu, bgu, wd, bd)


def _combine_kernel(h_ref, y_ref, gate_ref, g_ref, o_ref):
    tm, d = h_ref.shape
    nch = d // LANES
    gates = gate_ref[...]
    parts = []
    ss = None
    for c in range(nch):
        acc = h_ref[:, c * LANES:(c + 1) * LANES]
        for kk in range(TOP_K):
            acc = acc + gates[:, kk:kk + 1] * y_ref[kk, pl.ds(c, tm, stride=nch), :]
        parts.append(acc)
        sq = jnp.sum(acc * acc, axis=-1, keepdims=True)
        ss = sq if ss is None else ss + sq
    inv = lax.rsqrt(ss / (nch * LANES) + RMS_EPS)
    for c in range(nch):
        o_ref[:, c * LANES:(c + 1) * LANES] = parts[c] * inv * g_ref[:, c * LANES:(c + 1) * LANES]


def _combine(h, ytok3, gates, final_g):
    t, d = h.shape
    nch = d // LANES
    tm = min(TOKEN_TILE // 2, t)
    return pl.pallas_call(
        _combine_kernel,
        out_shape=jax.ShapeDtypeStruct((t, d), F32),
        grid=(t // tm,),
        in_specs=[pl.BlockSpec((tm, d), lambda i: (i, 0)),
                  pl.BlockSpec((TOP_K, tm * nch, LANES), lambda i: (0, i, 0)),
                  pl.BlockSpec((tm, TOP_K), lambda i: (i, 0)), pl.BlockSpec((1, d), lambda i: (0, 0))],
        out_specs=pl.BlockSpec((tm, d), lambda i: (i, 0)),
        compiler_params=pltpu.CompilerParams(dimension_semantics=("arbitrary",), vmem_limit_bytes=VMEM_LIMIT),
        name="combine",
    )(h, ytok3, gates, final_g.reshape(1, d).astype(F32))


def _moe_ffn(h, hn2, idx, gates, rank, counts, w_gate_up, b_gate_up, w_down, b_down, final_g):
    t, d = h.shape
    n_exp = w_gate_up.shape[0]
    d_ff = w_down.shape[1]
    rows = MOE_ROWS
    n_assign = t * TOP_K
    n_rows = n_assign + n_exp * rows
    n_blocks = n_rows // rows
    tok_pad = t + (2 * rows) // TOP_K

    cnt = counts[0, :n_exp].astype(I32)
    padded = (cnt + rows - 1) // rows * rows
    pad_ends = jnp.cumsum(padded)
    pad_starts = pad_ends - padded
    dest = pad_starts[idx] + rank
    dump = n_assign + jnp.arange(n_rows, dtype=I32) % (2 * rows)
    slot_rows = dump.at[dest.reshape(-1)].set(jnp.arange(n_assign, dtype=I32), unique_indices=True)
    blk_start = jnp.arange(n_blocks, dtype=I32) * rows
    blk_expert = jnp.minimum(jnp.sum((pad_ends[None, :] <= blk_start[:, None]).astype(I32), axis=1), n_exp - 1)
    n_used = (pad_ends[-1:] // rows).astype(I32)

    bgu = (b_gate_up.astype(F32).reshape(n_exp, d_ff // LANES, LANES, 2).transpose(0, 1, 3, 2)
           .reshape(n_exp, 1, 2 * d_ff))
    nch = d // LANES
    src_row = jnp.minimum(slot_rows >> 2, t - 1) * nch
    dst_row = ((slot_rows & (TOP_K - 1)) * tok_pad + (slot_rows >> 2)) * nch
    row_idx = jnp.stack([src_row.reshape(n_blocks, rows), dst_row.reshape(n_blocks, rows)], axis=1)
    ytok = _moe(hn2, row_idx, blk_expert, n_used, w_gate_up, bgu,
                w_down, b_down.reshape(n_exp, 1, d).astype(F32), t, tok_pad)
    return _combine(h, ytok.reshape(TOP_K, tok_pad * (d // LANES), LANES), gates, final_g)


def kernel(x, positions, ln_mix_g, w_in, lam_q1, lam_k1, lam_q2, lam_k2, diff_norm_g, ssm_lam_re, ssm_lam_im,
           ssm_b_re, ssm_b_im, ssm_c_re, ssm_c_im, ssm_d, ssm_log_dt, ssm_w_glu, ssm_b_glu, ssm_norm_g, w_out,
           ln_ffn_g, w_router, b_router, w_gate_up, b_gate_up, w_down, b_down, final_norm_g):
    bsz, s_len, d = x.shape
    depth = w_in.shape[0]
    assert depth == 1
    li = 0
    lambda_init = 0.8 - 0.6 * math.exp(-0.3 * li)
    d_attn = ATTN_HEADS * LANES
    d_ssm = w_in.shape[2] - 3 * d_attn
    t = bsz * s_len

    x2 = x.reshape(t, d)
    q, k, v, u, ut4 = _inproj(x2, positions.reshape(t, 1), ln_mix_g[li], w_in[li], d_attn, d_ssm, bsz)
    vt = v.reshape(bsz, s_len, d_attn).transpose(0, 2, 1)
    a_out = _attention(q.reshape(bsz, s_len, d_attn), k.reshape(bsz, s_len, d_attn), vt,
                       lam_q1[li], lam_k1[li], lam_q2[li], lam_k2[li], diff_norm_g[li], lambda_init)
    yt4 = _s5_core(ut4, ssm_lam_re[li], ssm_lam_im[li], ssm_b_re[li], ssm_b_im[li],
                   ssm_c_re[li], ssm_c_im[li], ssm_log_dt[li])
    h, hn2, idx, gates, rank, counts = _mix(
        x2, a_out.reshape(t, d_attn), yt4, u, ssm_d[li], ssm_w_glu[li], ssm_b_glu[li], ssm_norm_g[li],
        w_out[li], ln_ffn_g[li], w_router[li], b_router[li])
    out = _moe_ffn(h, hn2, idx, gates, rank, counts, w_gate_up[li], b_gate_up[li], w_down[li], b_down[li],
                   final_norm_g)
    return out.reshape(bsz, s_len, d)
```

```python
import functools
import math

import jax
import jax.numpy as jnp
from jax import lax
from jax.experimental import pallas as pl
from jax.experimental.pallas import tpu as pltpu

F32 = jnp.float32
BF16 = jnp.bfloat16
I32 = jnp.int32

RMS_EPS = 1e-6
ATTN_HEADS = 4
ROPE_DIM = 16
ROPE_THETA = 500000.0
SSM_GROUP = 16
SSM_STATE = 64
TOP_K = 4
SWIGLU_LIMIT = 7.0
SWIGLU_ALPHA = 1.702
LANES = 128
SUBLANES = 8
NEG_BIG = -1e30

TOKEN_TILE = 1024
MIX_SUBTILE = 512
ATTN_TILE = 512
ATTN_QUERY_SPLITS = 1
SHIFT_SLACK = 64.0
SSM_BATCH_PAD = SUBLANES
SCAN_BLOCK = 32
MOE_ROWS = 256
VMEM_LIMIT = 56 * 1024 * 1024


def _rms(x, g):
    return x * lax.rsqrt(jnp.mean(x * x, axis=-1, keepdims=True) + RMS_EPS) * g


def _dot_t(a, b, **kw):
    return lax.dot_general(a, b, (((1,), (1,)), ((), ())), preferred_element_type=F32, **kw)


def _inproj_kernel(x_ref, pos_ref, g_ref, w_ref, freq_ref, q_ref, k_ref, v_ref, u_ref, ut_ref, *, d_attn):
    x = x_ref[...]
    n = _rms(x, g_ref[...]).astype(BF16)
    proj = jnp.dot(n, w_ref[...], preferred_element_type=F32)
    ang = pos_ref[...].astype(F32) * freq_ref[...]
    lane = lax.broadcasted_iota(I32, (1, LANES), 1) & (LANES // 2 - 1)
    half = ROPE_DIM // 2
    first = lane < half
    second = jnp.logical_and(lane >= half, lane < ROPE_DIM)
    cos = jnp.cos(ang)
    sin = jnp.sin(ang)
    cm = jnp.where(lane < ROPE_DIM, cos, 1.0)
    sa = jnp.where(first, -sin, 0.0)
    sb = jnp.where(second, sin, 0.0)

    def rope(t):
        return t * cm + pltpu.roll(t, LANES - half, 1) * sa + pltpu.roll(t, half, 1) * sb

    scale = (LANES // 2) ** -0.5 * math.log2(math.e)
    for h in range(d_attn // LANES):
        sl = slice(h * LANES, (h + 1) * LANES)
        q_ref[:, sl] = (rope(proj[:, sl]) * scale).astype(BF16)
        k_ref[:, sl] = rope(proj[:, d_attn + h * LANES:d_attn + (h + 1) * LANES]).astype(BF16)
    v_ref[...] = proj[:, 2 * d_attn:3 * d_attn].astype(BF16)
    u = proj[:, 3 * d_attn:]
    u_ref[...] = u.astype(BF16)
    ut = u.T
    for c in range(ut_ref.shape[1]):
        ut_ref[:, c, :] = ut[:, c * LANES:(c + 1) * LANES]


def _inproj(x2, pos2, g, w_in, d_attn, d_ssm, bsz):
    t, d = x2.shape
    s_len = t // bsz
    tm = min(TOKEN_TILE, s_len)
    tiles = s_len // tm
    cpt = tm // LANES
    assert s_len % tm == 0 and cpt % SUBLANES == 0
    n_in = w_in.shape[1]
    inv_freq = ROPE_THETA ** (-jnp.arange(0, ROPE_DIM, 2, dtype=F32) / ROPE_DIM)
    lane = jnp.arange(LANES) % (LANES // 2)
    freq = jnp.where(lane < ROPE_DIM, inv_freq[lane % (ROPE_DIM // 2)], 0.0).reshape(1, LANES)
    row = lambda i: (i, 0)
    const = lambda i: (0, 0)
    return pl.pallas_call(
        functools.partial(_inproj_kernel, d_attn=d_attn),
        out_shape=(jax.ShapeDtypeStruct((t, d_attn), BF16),) * 3 + (
            jax.ShapeDtypeStruct((t, d_ssm), BF16), jax.ShapeDtypeStruct((bsz, d_ssm, s_len // LANES, LANES), F32)),
        grid=(t // tm,),
        in_specs=[pl.BlockSpec((tm, d), row), pl.BlockSpec((tm, 1), row), pl.BlockSpec((1, d), const),
                  pl.BlockSpec((d, n_in), const), pl.BlockSpec((1, LANES), const)],
        out_specs=(pl.BlockSpec((tm, d_attn), row),) * 3 + (
            pl.BlockSpec((tm, d_ssm), row),
            pl.BlockSpec((None, d_ssm, cpt, LANES), lambda i: (i // tiles, 0, i % tiles, 0))),
        compiler_params=pltpu.CompilerParams(dimension_semantics=("arbitrary",), vmem_limit_bytes=VMEM_LIMIT),
        name="inproj",
    )(x2, pos2, g.reshape(1, d), w_in.astype(BF16), freq)


def _attn_kernel(q_ref, k_ref, vt_ref, lq1_ref, lk1_ref, lq2_ref, lk2_ref, g_ref, o_ref, acc1, acc2, *,
                 tile, lambda_init):
    qi = pl.program_id(2)
    q = q_ref[...]
    lane = lax.broadcasted_iota(I32, (1, LANES), 1)
    zero = jnp.zeros_like(q)
    q_maps = (jnp.where(lane < LANES // 2, q, zero), jnp.where(lane >= LANES // 2, q, zero))
    accs = (acc1, acc2)
    acc1[...] = jnp.zeros_like(acc1)
    acc2[...] = jnp.zeros_like(acc2)
    half = tile // ATTN_QUERY_SPLITS
    units = [(mi, h) for mi in range(2) for h in range(ATTN_QUERY_SPLITS)]
    m0 = jnp.full((1, half), NEG_BIG, F32)
    l0 = jnp.zeros((1, half), F32)

    def scores(j, unit, masked):
        mi, h = unit
        k = k_ref[pl.ds(pl.multiple_of(j * tile, tile), tile), :]
        s = _dot_t(k, q_maps[mi][h * half:(h + 1) * half])
        if masked:
            keep = (lax.broadcasted_iota(I32, (tile, half), 0)
                    <= lax.broadcasted_iota(I32, (tile, half), 1) + h * half)
            s = jnp.where(keep, s, NEG_BIG)
        return s

    def values_t(j):
        return vt_ref[:, pl.ds(pl.multiple_of(j * tile, tile), tile)]

    def rescaling_step(j, carry, masked):
        vt = values_t(j)
        out = []
        all_scores = [scores(j, unit, masked) for unit in units]
        for ui, (mi, h) in enumerate(units):
            m, l = carry[2 * ui], carry[2 * ui + 1]
            cols = slice(h * half, (h + 1) * half)
            s = all_scores[ui]
            m_new = jnp.maximum(m, jnp.max(s, axis=0, keepdims=True))
            alpha = jnp.exp2(m - m_new)
            p = jnp.exp2(s - m_new)
            l_new = alpha * l + jnp.sum(p, axis=0, keepdims=True)
            accs[mi][:, cols] = (accs[mi][:, cols] * alpha
                                 + jnp.dot(vt, p.astype(BF16), preferred_element_type=F32))
            out += [m_new, l_new]
        return tuple(out)

    def step(blocks, carry):
        pend = [None] * (2 * len(units))
        worst = None
        ready = []
        for blk in list(blocks) + [None]:
            fresh = [] if blk is None else [(scores(blk[0], unit, blk[1]), values_t(blk[0])) for unit in units]
            for ui, (s, vt) in enumerate(ready):
                m = carry[2 * ui]
                p = jnp.exp2(s - m)
                over = jnp.max(s, axis=0, keepdims=True) - m
                worst = over if worst is None else jnp.maximum(worst, over)
                new = (jnp.dot(vt, p.astype(BF16), preferred_element_type=F32), jnp.sum(p, axis=0, keepdims=True))
                for t in range(2):
                    pend[2 * ui + t] = new[t] if pend[2 * ui + t] is None else pend[2 * ui + t] + new[t]
            ready = fresh

        def keep_shift():
            out = []
            for ui, (mi, h) in enumerate(units):
                cols = slice(h * half, (h + 1) * half)
                accs[mi][:, cols] = accs[mi][:, cols] + pend[2 * ui]
                out += [carry[2 * ui], carry[2 * ui + 1] + pend[2 * ui + 1]]
            return tuple(out)

        def rescale():
            c = carry
            for j, masked in blocks:
                c = rescaling_step(j, c, masked)
            return c

        return lax.cond(jnp.max(worst) <= SHIFT_SLACK, keep_shift, rescale)

    init = (m0, l0) * len(units)
    carry = lax.cond(qi == 0, lambda: rescaling_step(0, init, True), lambda: rescaling_step(0, init, False))
    plain = jnp.maximum(qi - 1, 0)
    triples = plain // 3
    carry = lax.fori_loop(
        0, triples, lambda t, c: step([(1 + 3 * t, False), (2 + 3 * t, False), (3 + 3 * t, False)], c), carry)
    rest = 1 + 3 * triples
    left = plain - 3 * triples
    carry = lax.cond(left == 2, lambda: step([(rest, False), (rest + 1, False)], carry), lambda: carry)
    carry = lax.cond(left == 1, lambda: step([(rest, False)], carry), lambda: carry)
    carry = lax.cond(qi > 0, lambda: step([(qi, True)], carry), lambda: carry)
    ns = ATTN_QUERY_SPLITS
    l1 = jnp.concatenate([carry[2 * ui + 1] for ui in range(ns)], axis=-1)
    l2 = jnp.concatenate([carry[2 * ui + 1] for ui in range(ns, 2 * ns)], axis=-1)

    lam = (jnp.exp(jnp.sum(lq1_ref[...] * lk1_ref[...], axis=-1, keepdims=True))
           - jnp.exp(jnp.sum(lq2_ref[...] * lk2_ref[...], axis=-1, keepdims=True)) + lambda_init)
    ot = acc1[...] / l1 - lam * (acc2[...] / l2)
    o = _rms(ot.T, g_ref[...]) * (1.0 - lambda_init)
    o_ref[...] = o.astype(o_ref.dtype)


def _attention(q, k, vt, lq1, lk1, lq2, lk2, norm_g, lambda_init):
    b, s, d_attn = q.shape
    heads = d_attn // LANES
    tile = min(ATTN_TILE, s)
    dk = lq1.shape[-1]
    vec = lambda a: a.reshape(1, -1).astype(F32)
    const = lambda bi, h, qi: (0, 0)
    return pl.pallas_call(
        functools.partial(_attn_kernel, tile=tile, lambda_init=lambda_init),
        out_shape=jax.ShapeDtypeStruct((b, s, d_attn), BF16),
        grid=(b, heads, s // tile),
        in_specs=[pl.BlockSpec((None, tile, LANES), lambda bi, h, qi: (bi, qi, h)),
                  pl.BlockSpec((None, s, LANES), lambda bi, h, qi: (bi, 0, h)),
                  pl.BlockSpec((None, LANES, s), lambda bi, h, qi: (bi, h, 0)),
                  pl.BlockSpec((1, dk), const), pl.BlockSpec((1, dk), const),
                  pl.BlockSpec((1, dk), const), pl.BlockSpec((1, dk), const),
                  pl.BlockSpec((1, LANES), const)],
        out_specs=pl.BlockSpec((None, tile, LANES), lambda bi, h, qi: (bi, qi, h)),
        scratch_shapes=[pltpu.VMEM((LANES, tile), F32), pltpu.VMEM((LANES, tile), F32)],
        compiler_params=pltpu.CompilerParams(dimension_semantics=("arbitrary",) * 3,
                                             vmem_limit_bytes=VMEM_LIMIT),
        name="attn",
    )(q, k, vt, vec(lq1), vec(lk1), vec(lq2), vec(lk2), vec(norm_g))


def _ssm_consts(lre, lim, ldt):
    dt = jnp.exp(ldt)
    return lre * dt, lim * dt


def _cpow(e, ldr, ldi):
    mag = jnp.exp(e * ldr)
    ang = e * ldi
    return mag * jnp.cos(ang), mag * jnp.sin(ang)


def _bbar(lre, lim, ldr, ldi, bre, bim):
    mag = jnp.exp(ldr)
    nr = mag * jnp.cos(ldi) - 1.0
    ni = mag * jnp.sin(ldi)
    den = lre * lre + lim * lim
    cr = (nr * lre + ni * lim) / den
    ci = (ni * lre - nr * lim) / den
    return cr * bre - ci * bim, cr * bim + ci * bre


def _crows(pr, pi, xr, xi):
    nc = xr.shape[0]
    rr = jnp.concatenate([pr * xr[c:c + 1] - pi * xi[c:c + 1] for c in range(nc)], axis=0)
    ri = jnp.concatenate([pr * xi[c:c + 1] + pi * xr[c:c + 1] for c in range(nc)], axis=0)
    return rr, ri


def _chunk_rows(ut_ref, gi, uc3):
    bsz, nc2, kc, _ = ut_ref.shape
    nc = nc2 // 2
    for b in range(bsz):
        for c in range(nc):
            uc3[:, b, c * LANES:(c + 1) * LANES] = ut_ref[b, gi * nc + c]
    return uc3[...].reshape(kc * uc3.shape[1], nc * LANES).astype(BF16)


def _s5_local_kernel(ut_ref, lre_ref, lim_ref, ldt_ref, bre_ref, bim_ref, zr_ref, zi_ref, uc3, *, chunk):
    @pl.when(pl.program_id(0) == 0)
    def _():
        uc3[...] = jnp.zeros_like(uc3)

    lane = lax.broadcasted_iota(I32, (1, LANES), 1)
    e = (chunk - 1.0) - lax.broadcasted_iota(I32, (chunk, 1), 0).astype(F32)
    zr = None
    for gi in range(2):
        lre, lim = lre_ref[gi], lim_ref[gi]
        ldr, ldi = _ssm_consts(lre, lim, ldt_ref[gi])
        bbr, bbi = _bbar(lre, lim, ldr, ldi, bre_ref[gi], bim_ref[gi])
        pr, pi = _cpow(e, ldr, ldi)
        wr, wi = _crows(pr, pi, bbr, bbi)
        mine = (lane < LANES // 2) if gi == 0 else (lane >= LANES // 2)
        u = _chunk_rows(ut_ref, gi, uc3)
        pr_ = jnp.dot(u, jnp.where(mine, wr, 0.0).astype(BF16), preferred_element_type=F32)
        pi_ = jnp.dot(u, jnp.where(mine, wi, 0.0).astype(BF16), preferred_element_type=F32)
        zr, zi = (pr_, pi_) if zr is None else (zr + pr_, zi + pi_)
    zr_ref[...] = zr
    zi_ref[...] = zi


def _s5_scan_kernel(zr_ref, zi_ref, lre_ref, lim_ref, ldt_ref, xr_ref, xi_ref, sr, si, ar, ai, *, chunk, nblk):
    @pl.when(pl.program_id(0) == 0)
    def _():
        ldr, ldi = _ssm_consts(lre_ref[...], lim_ref[...], ldt_ref[...])
        mag = jnp.exp(chunk * ldr)
        ar[...] = jnp.broadcast_to(mag * jnp.cos(chunk * ldi), ar.shape)
        ai[...] = jnp.broadcast_to(mag * jnp.sin(chunk * ldi), ai.shape)
        sr[...] = jnp.zeros_like(sr)
        si[...] = jnp.zeros_like(si)

    a_r = ar[...]
    a_i = ai[...]

    def body(kk, carry):
        xr, xi = carry
        rows = pl.ds(pl.multiple_of(kk * SSM_BATCH_PAD, SSM_BATCH_PAD), SSM_BATCH_PAD)
        xr_ref[rows, :] = xr
        xi_ref[rows, :] = xi
        return (a_r * xr - a_i * xi + zr_ref[rows, :], a_r * xi + a_i * xr + zi_ref[rows, :])

    xr, xi = lax.fori_loop(0, nblk, body, (sr[...], si[...]))
    sr[...] = xr
    si[...] = xi


def _s5_out_kernel(ut_ref, xr_ref, xi_ref, lre_ref, lim_ref, ldt_ref, bre_ref, bim_ref, cre_ref, cim_ref,
                   yt_ref, uc3, y3, toep, kern, *, chunk):
    @pl.when(pl.program_id(0) == 0)
    def _():
        uc3[...] = jnp.zeros_like(uc3)

    bsz, nc2, kc, _ = ut_ref.shape
    nc = nc2 // 2
    lane = lax.broadcasted_iota(I32, (1, LANES), 1)
    low = lane < LANES // 2
    step = lax.broadcasted_iota(I32, (chunk, 1), 0).astype(F32)
    causal = lax.broadcasted_iota(I32, (chunk, chunk), 1) >= lax.broadcasted_iota(I32, (chunk, chunk), 0)
    xr = xr_ref[...].astype(BF16)
    xi = xi_ref[...].astype(BF16)
    for gi in range(2):
        lre, lim = lre_ref[gi], lim_ref[gi]
        ldr, ldi = _ssm_consts(lre, lim, ldt_ref[gi])
        bbr, bbi = _bbar(lre, lim, ldr, ldi, bre_ref[gi], bim_ref[gi])
        cre, cim = cre_ref[gi], cim_ref[gi]
        pr, pi = _cpow(step, ldr, ldi)
        gr, gim = _crows(pr, pi, cre, cim)
        kmat = _dot_t(jnp.where(low, bbr, -bbi), jnp.where(low, gr, gim), precision=lax.Precision.HIGHEST)
        for c1 in range(nc):
            kern[c1] = kmat[c1:c1 + 1]

        def toeplitz_rows(c1, _):
            rows = pl.ds(pl.multiple_of(c1 * chunk, chunk), chunk)
            for c in range(nc):
                resp = jnp.broadcast_to(kern[c1, :, c * chunk:(c + 1) * chunk], (chunk, chunk))
                blk = pltpu.roll(resp, 0, 1, stride=1, stride_axis=0)
                toep[rows, c * chunk:(c + 1) * chunk] = jnp.where(causal, blk, 0.0).astype(BF16)
            return 0

        lax.fori_loop(0, nc, toeplitz_rows, 0)
        qr, qi = _cpow(step + 1.0, ldr, ldi)
        vr, vi = _crows(qr, qi, cre, cim)
        mine = low if gi == 0 else jnp.logical_not(low)
        y = jnp.dot(_chunk_rows(ut_ref, gi, uc3), toep[...], preferred_element_type=F32)
        y = y + _dot_t(xr, jnp.where(mine, vr, 0.0).astype(BF16))
        y = y - _dot_t(xi, jnp.where(mine, vi, 0.0).astype(BF16))
        y3[...] = y.reshape(y3.shape)
        for b in range(bsz):
            for c in range(nc):
                yt_ref[b, gi * nc + c] = y3[:, b, c * LANES:(c + 1) * LANES]


def _s5_core(ut4, lam_re, lam_im, b_re, b_im, c_re, c_im, log_dt):
    bsz, _, kc, chunk = ut4.shape
    g, n = lam_re.shape
    nc = SSM_GROUP
    bp = SSM_BATCH_PAD
    rows = kc * bp
    r = nc * chunk
    assert 2 * n == LANES and chunk == LANES and g % 2 == 0 and bsz <= bp and kc % min(SCAN_BLOCK, kc) == 0

    dup = lambda a: jnp.concatenate([a, a], axis=-1).astype(F32)
    lre2 = dup(lam_re)[:, None, :]
    lim2 = dup(lam_im)[:, None, :]
    ldt2 = jnp.broadcast_to(log_dt.astype(F32)[:, None, None], (g, 1, LANES))
    bre2 = dup(b_re.transpose(0, 2, 1))
    bim2 = dup(b_im.transpose(0, 2, 1))
    cre2 = dup(c_re)
    cim2 = dup(c_im)

    pair3 = lambda p: (p, 0, 0)
    vec_spec = pl.BlockSpec((2, 1, LANES), pair3)
    coef_spec = pl.BlockSpec((2, nc, LANES), pair3)
    u_spec = pl.BlockSpec((bsz, 2 * nc, kc, chunk), lambda p: (0, p, 0, 0))
    z_spec = pl.BlockSpec((rows, LANES), lambda p: (0, p))
    params = pltpu.CompilerParams(dimension_semantics=("arbitrary",), vmem_limit_bytes=VMEM_LIMIT)
    rows3 = pltpu.VMEM((kc, bp, r), F32)

    zr, zi = pl.pallas_call(
        functools.partial(_s5_local_kernel, chunk=chunk),
        out_shape=(jax.ShapeDtypeStruct((rows, g * n), F32),) * 2,
        grid=(g // 2,),
        in_specs=[u_spec, vec_spec, vec_spec, vec_spec, coef_spec, coef_spec],
        out_specs=(z_spec, z_spec),
        scratch_shapes=[rows3],
        compiler_params=params, name="s5_local",
    )(ut4, lre2, lim2, ldt2, bre2, bim2)

    nblk = min(SCAN_BLOCK, kc)
    flat = lambda a: a.astype(F32).reshape(1, g * n)
    ldt_flat = jnp.broadcast_to(log_dt.astype(F32)[:, None], (g, n)).reshape(1, g * n)
    blk = pl.BlockSpec((nblk * bp, g * n), lambda i: (i, 0))
    cst = pl.BlockSpec((1, g * n), lambda i: (0, 0))
    xr, xi = pl.pallas_call(
        functools.partial(_s5_scan_kernel, chunk=chunk, nblk=nblk),
        out_shape=(jax.ShapeDtypeStruct((rows, g * n), F32),) * 2,
        grid=(kc // nblk,),
        in_specs=[blk, blk, cst, cst, cst],
        out_specs=(blk, blk),
        scratch_shapes=[pltpu.VMEM((bp, g * n), F32)] * 4,
        compiler_params=params, name="s5_scan",
    )(zr, zi, flat(lam_re), flat(lam_im), ldt_flat)

    return pl.pallas_call(
        functools.partial(_s5_out_kernel, chunk=chunk),
        out_shape=jax.ShapeDtypeStruct(ut4.shape, F32),
        grid=(g // 2,),
        in_specs=[u_spec, z_spec, z_spec, vec_spec, vec_spec, vec_spec, coef_spec, coef_spec, coef_spec, coef_spec],
        out_specs=u_spec,
        scratch_shapes=[rows3, rows3, pltpu.VMEM((r, r), BF16), pltpu.VMEM((nc, 1, r), F32)],
        compiler_params=params, name="s5_out",
    )(ut4, xr, xi, lre2, lim2, ldt2, bre2, bim2, cre2, cim2)


def _mix_kernel(x_ref, a_ref, y_ref, u_ref, d_ref, wglu_ref, bglu_ref, gs_ref, wout_ref, gffn_ref,
                wr_ref, br_ref, h_ref, hn_ref, idx_ref, gate_ref, rank_ref, cnt_ref, run_ref, *, d_attn):
    i = pl.program_id(0)

    @pl.when(i == 0)
    def _():
        run_ref[...] = jnp.zeros_like(run_ref)

    tm = min(MIX_SUBTILE, x_ref.shape[0])
    nch = x_ref.shape[1] // LANES
    for sub in range(x_ref.shape[0] // tm):
        _mix_subtile(sub, tm, nch, d_attn, x_ref, a_ref, y_ref, u_ref, d_ref, wglu_ref, bglu_ref, gs_ref,
                     wout_ref, gffn_ref, wr_ref, br_ref, h_ref, hn_ref, idx_ref, gate_ref, rank_ref, run_ref)
    cnt_ref[...] = run_ref[...]


def _mix_subtile(sub, tm, nch, d_attn, x_ref, a_ref, y_ref, u_ref, d_ref, wglu_ref, bglu_ref, gs_ref, wout_ref,
                 gffn_ref, wr_ref, br_ref, h_ref, hn_ref, idx_ref, gate_ref, rank_ref, run_ref):
    rows = pl.ds(sub * tm, tm)
    cps = tm // LANES
    yt = jnp.concatenate([y_ref[:, sub * cps + c, :] for c in range(cps)], axis=-1)
    y = yt.T + d_ref[...] * u_ref[rows, :].astype(F32)
    y = 0.5 * y * (1.0 + lax.erf(y * (0.5 ** 0.5)))
    z = jnp.dot(y.astype(BF16), wglu_ref[...], preferred_element_type=F32) + bglu_ref[...]
    y = y * jax.nn.sigmoid(z)
    s_out = _rms(y, gs_ref[...])
    mix = (jnp.dot(a_ref[rows, :], wout_ref[:d_attn, :], preferred_element_type=F32)
           + jnp.dot(s_out.astype(BF16), wout_ref[d_attn:, :], preferred_element_type=F32))
    h = x_ref[rows, :] + mix
    h_ref[rows, :] = h
    hn = _rms(h, gffn_ref[...])
    for c in range(nch):
        hn_ref[pl.ds(sub * tm * nch + c, tm, stride=nch), :] = hn[:, c * LANES:(c + 1) * LANES]

    lg = jnp.dot(hn, wr_ref[...], preferred_element_type=F32, precision=lax.Precision.HIGHEST) + br_ref[...]
    lane = lax.broadcasted_iota(I32, (tm, LANES), 1)
    lane_f = lane.astype(F32)
    vals, hots = [], []
    idx_all = jnp.zeros((tm, LANES), F32)
    for kk in range(TOP_K):
        mx = jnp.max(lg, axis=-1, keepdims=True)
        ix = jnp.min(jnp.where(lg == mx, lane_f, float(LANES)), axis=-1, keepdims=True)
        hot = lane_f == ix
        vals.append(mx)
        hots.append(hot)
        idx_all = jnp.where(lane == kk, ix, idx_all)
        lg = jnp.where(hot, -3e38, lg)
    ex = [jnp.exp(v - vals[0]) for v in vals]
    den = ex[0] + ex[1] + ex[2] + ex[3]
    gate_all = jnp.zeros((tm, LANES), F32)
    for kk in range(TOP_K):
        gate_all = jnp.where(lane == kk, ex[kk] / den, gate_all)

    picked = jnp.zeros((tm, LANES), F32)
    for hot in hots:
        picked = jnp.where(hot, 1.0, picked)
    tri = (lax.broadcasted_iota(I32, (tm, tm), 1) < lax.broadcasted_iota(I32, (tm, tm), 0))
    before = jnp.dot(jnp.where(tri, 1.0, 0.0).astype(BF16), picked.astype(BF16), preferred_element_type=F32)
    before = before + run_ref[...]
    rank_all = jnp.zeros((tm, LANES), F32)
    for kk, hot in enumerate(hots):
        rk = jnp.sum(jnp.where(hot, before, 0.0), axis=-1, keepdims=True)
        rank_all = jnp.where(lane == kk, rk, rank_all)
    run_ref[...] = run_ref[...] + jnp.sum(picked, axis=0, keepdims=True)
    idx_ref[rows, :] = idx_all[:, :TOP_K].astype(I32)
    gate_ref[rows, :] = gate_all[:, :TOP_K]
    rank_ref[rows, :] = rank_all[:, :TOP_K].astype(I32)


def _mix(x2, a_out, yt4, u, ssm_d, w_glu, b_glu, ssm_norm_g, w_out, ln_ffn_g, w_router, b_router):
    t, d = x2.shape
    d_attn = a_out.shape[1]
    bsz, d_ssm, kc, _ = yt4.shape
    s_len = t // bsz
    n_exp = w_router.shape[1]
    tm = min(TOKEN_TILE, s_len)
    tiles = s_len // tm
    nch = d // LANES
    wr = jnp.zeros((d, LANES), F32).at[:, :n_exp].set(w_router.astype(F32))
    br = jnp.full((1, LANES), NEG_BIG, F32).at[0, :n_exp].set(b_router.astype(F32))
    row = lambda i: (i, 0)
    const = lambda i: (0, 0)
    vec = lambda a: a.reshape(1, -1).astype(F32)
    return pl.pallas_call(
        functools.partial(_mix_kernel, d_attn=d_attn),
        out_shape=(jax.ShapeDtypeStruct((t, d), F32), jax.ShapeDtypeStruct((t * nch, LANES), F32),
                   jax.ShapeDtypeStruct((t, TOP_K), I32), jax.ShapeDtypeStruct((t, TOP_K), F32),
                   jax.ShapeDtypeStruct((t, TOP_K), I32), jax.ShapeDtypeStruct((1, LANES), F32)),
        grid=(t // tm,),
        in_specs=[pl.BlockSpec((tm, d), row), pl.BlockSpec((tm, d_attn), row),
                  pl.BlockSpec((None, d_ssm, tm // LANES, LANES), lambda i: (i // tiles, 0, i % tiles, 0)),
                  pl.BlockSpec((tm, d_ssm), row), pl.BlockSpec((1, d_ssm), const),
                  pl.BlockSpec((d_ssm, d_ssm), const), pl.BlockSpec((1, d_ssm), const),
                  pl.BlockSpec((1, d_ssm), const), pl.BlockSpec((d_attn + d_ssm, d), const),
                  pl.BlockSpec((1, d), const), pl.BlockSpec((d, LANES), const), pl.BlockSpec((1, LANES), const)],
        out_specs=(pl.BlockSpec((tm, d), row), pl.BlockSpec((tm * nch, LANES), row),
                   pl.BlockSpec((tm, TOP_K), row), pl.BlockSpec((tm, TOP_K), row), pl.BlockSpec((tm, TOP_K), row),
                   pl.BlockSpec((1, LANES), const)),
        scratch_shapes=[pltpu.VMEM((1, LANES), F32)],
        compiler_params=pltpu.CompilerParams(dimension_semantics=("arbitrary",), vmem_limit_bytes=VMEM_LIMIT),
        name="mix",
    )(x2, a_out, yt4, u, vec(ssm_d), w_glu.astype(BF16), vec(b_glu), vec(ssm_norm_g), w_out.astype(BF16),
      vec(ln_ffn_g), wr, br)


IDX_RING = 4


def _moe_kernel(blk_e_ref, nused_ref, slot_hbm, hn_hbm, wgu_ref, bgu_ref, wd_ref, bd_ref, ytok_hbm,
                idx_smem, xbuf, ybuf, wperm, wdown, idx_sem, g_sem, s_sem, *,
                n_tok, n_blocks, tok_pad, rows, nch):
    i = pl.program_id(0)
    n_used = nused_ref[0]
    d_ff = wdown.shape[0]
    grp = 2 * LANES

    def idx_copy(blk, src=None):
        src = jnp.minimum(blk, n_blocks - 1) if src is None else src
        slot = blk & (IDX_RING - 1)
        return pltpu.make_async_copy(slot_hbm.at[src], idx_smem.at[slot], idx_sem.at[slot])

    blk_rows = rows * nch

    def half(par):
        return pl.ds(par * blk_rows, blk_rows)

    def gather_wait(par):
        pltpu.make_async_copy(hn_hbm.at[pl.ds(0, blk_rows)], xbuf.at[half(par)], g_sem.at[par]).wait()

    def scatter_wait(par):
        pltpu.make_async_copy(ybuf.at[half(par)], ytok_hbm.at[pl.ds(0, blk_rows)], s_sem.at[par]).wait()

    def issue_gather(blk, par):
        slot = blk & (IDX_RING - 1)
        for r in range(rows):
            src = pl.ds(pl.multiple_of(idx_smem[slot, 0, r], nch), nch)
            pltpu.make_async_copy(hn_hbm.at[src], xbuf.at[pl.ds(par * blk_rows + r * nch, nch)],
                                  g_sem.at[par]).start(priority=r % 2)

    def issue_scatter(blk, par):
        slot = blk & (IDX_RING - 1)
        for r in range(rows):
            dst = pl.ds(pl.multiple_of(idx_smem[slot, 1, r], nch), nch)
            pltpu.make_async_copy(ybuf.at[pl.ds(par * blk_rows + r * nch, nch)], ytok_hbm.at[dst],
                                  s_sem.at[par]).start(priority=r % 2)

    @pl.when(i == 0)
    def _():
        idx_copy(0).start()
        idx_copy(1).start()
        idx_copy(-1, src=n_blocks - 1).start()
        spare = (tok_pad - n_tok) * nch
        ybuf[...] = jnp.zeros_like(ybuf)
        fills = [pltpu.make_async_copy(ybuf.at[pl.ds(0, spare)],
                                       ytok_hbm.at[pl.ds((kk * tok_pad + n_tok) * nch, spare)], s_sem.at[0])
                 for kk in range(TOP_K)]
        for f in fills:
            f.start()
        for f in fills:
            f.wait()
        idx_copy(0).wait()
        idx_copy(-1, src=n_blocks - 1).wait()
        issue_gather(0, 0)

    def block(par):
        idx_copy(i + 1).wait()
        gather_wait(par)

        @pl.when(i > 0)
        def _():
            scatter_wait(par)

        expert = blk_e_ref[i]

        @pl.when(jnp.logical_or(i == 0, expert != blk_e_ref[jnp.maximum(i - 1, 0)]))
        def _():
            src = lax.broadcasted_iota(I32, (grp, grp), 0)
            col = lax.broadcasted_iota(I32, (grp, grp), 1)
            want = jnp.where(col < LANES, 2 * col, 2 * (col - LANES) + 1)
            perm = jnp.where(src == want, 1.0, 0.0).astype(BF16)
            for gi in range(wperm.shape[1] // grp):
                cols = slice(gi * grp, (gi + 1) * grp)
                wperm[:, cols] = jnp.dot(wgu_ref[:, cols].astype(BF16), perm,
                                         preferred_element_type=F32).astype(BF16)
            wdown[...] = wd_ref[...].astype(BF16)

        issue_gather(i + 1, 1 - par)
        issue_scatter(i - 1, 1 - par)
        x = jnp.concatenate([xbuf[pl.ds(par * blk_rows + c, rows, stride=nch), :] for c in range(nch)],
                            axis=-1).astype(BF16)
        gu = jnp.dot(x, wperm[...], preferred_element_type=F32) + bgu_ref[...]
        acts = []
        for gi in range(d_ff // LANES):
            gate = jnp.minimum(gu[:, gi * grp:gi * grp + LANES], SWIGLU_LIMIT)
            up = jnp.clip(gu[:, gi * grp + LANES:(gi + 1) * grp], -SWIGLU_LIMIT, SWIGLU_LIMIT)
            acts.append((gate * jax.nn.sigmoid(SWIGLU_ALPHA * gate) * (up + 1.0)).astype(BF16))
        y = jnp.dot(jnp.concatenate(acts, axis=-1), wdown[...], preferred_element_type=F32) + bd_ref[...]
        for c in range(nch):
            ybuf[pl.ds(par * blk_rows + c, rows, stride=nch), :] = y[:, c * LANES:(c + 1) * LANES]
        idx_copy(i + 2).start()

        @pl.when(i == n_used - 1)
        def _():
            issue_scatter(i, par)
            scatter_wait(1 - par)
            scatter_wait(par)
            gather_wait(1 - par)
            idx_copy(i + 2).wait()

    for par in range(2):
        pl.when(jnp.logical_and(i < n_used, (i & 1) == par))(functools.partial(block, par))


def _moe(hn2, slot_rows, blk_expert, n_used, wgu, bgu, wd, bd, n_tok, tok_pad):
    n_blocks, _, rows = slot_rows.shape
    n_exp, d, d_ff2 = wgu.shape
    d_ff = d_ff2 // 2
    nch = d // LANES
    wmap = lambda i, be, nu: (be[i], 0, 0)
    return pl.pallas_call(
        functools.partial(_moe_kernel, n_tok=n_tok, n_blocks=n_blocks, tok_pad=tok_pad, rows=rows, nch=nch),
        out_shape=jax.ShapeDtypeStruct((TOP_K * tok_pad * nch, LANES), F32),
        grid_spec=pltpu.PrefetchScalarGridSpec(
            num_scalar_prefetch=2,
            grid=(n_blocks,),
            in_specs=[pl.BlockSpec(memory_space=pl.ANY), pl.BlockSpec(memory_space=pl.ANY),
                      pl.BlockSpec((None, d, d_ff2), wmap), pl.BlockSpec((None, 1, d_ff2), wmap),
                      pl.BlockSpec((None, d_ff, d), wmap), pl.BlockSpec((None, 1, d), wmap)],
            out_specs=pl.BlockSpec(memory_space=pl.ANY),
            scratch_shapes=[pltpu.SMEM((IDX_RING, 2, rows), I32), pltpu.VMEM((2 * rows * nch, LANES), F32),
                            pltpu.VMEM((2 * rows * nch, LANES), F32), pltpu.VMEM((d, d_ff2), BF16),
                            pltpu.VMEM((d_ff, d), BF16), pltpu.SemaphoreType.DMA((IDX_RING,)),
                            pltpu.SemaphoreType.DMA((2,)), pltpu.SemaphoreType.DMA((2,))]),
        compiler_params=pltpu.CompilerParams(dimension_semantics=("arbitrary",), vmem_limit_bytes=VMEM_LIMIT),
        name="moe",
    )(blk_expert, n_used, slot_rows, hn2, wgu, bgu, wd, bd)


def _combine_kernel(h_ref, y_ref, gate_ref, g_ref, o_ref):
    tm, d = h_ref.shape
    nch = d // LANES
    gates = gate_ref[...]
    parts = []
    ss = None
    for c in range(nch):
        acc = h_ref[:, c * LANES:(c + 1) * LANES]
        for kk in range(TOP_K):
            acc = acc + gates[:, kk:kk + 1] * y_ref[kk, pl.ds(c, tm, stride=nch), :]
        parts.append(acc)
        sq = jnp.sum(acc * acc, axis=-1, keepdims=True)
        ss = sq if ss is None else ss + sq
    inv = lax.rsqrt(ss / (nch * LANES) + RMS_EPS)
    for c in range(nch):
        o_ref[:, c * LANES:(c + 1) * LANES] = parts[c] * inv * g_ref[:, c * LANES:(c + 1) * LANES]


def _combine(h, ytok3, gates, final_g):
    t, d = h.shape
    nch = d // LANES
    tm = min(TOKEN_TILE // 2, t)
    return pl.pallas_call(
        _combine_kernel,
        out_shape=jax.ShapeDtypeStruct((t, d), F32),
        grid=(t // tm,),
        in_specs=[pl.BlockSpec((tm, d), lambda i: (i, 0)),
                  pl.BlockSpec((TOP_K, tm * nch, LANES), lambda i: (0, i, 0)),
                  pl.BlockSpec((tm, TOP_K), lambda i: (i, 0)), pl.BlockSpec((1, d), lambda i: (0, 0))],
        out_specs=pl.BlockSpec((tm, d), lambda i: (i, 0)),
        compiler_params=pltpu.CompilerParams(dimension_semantics=("arbitrary",), vmem_limit_bytes=VMEM_LIMIT),
        name="combine",
    )(h, ytok3, gates, final_g.reshape(1, d).astype(F32))


def _moe_ffn(h, hn2, idx, gates, rank, counts, w_gate_up, b_gate_up, w_down, b_down, final_g):
    t, d = h.shape
    n_exp = w_gate_up.shape[0]
    d_ff = w_down.shape[1]
    rows = MOE_ROWS
    n_assign = t * TOP_K
    n_rows = n_assign + n_exp * rows
    n_blocks = n_rows // rows
    tok_pad = t + (2 * rows) // TOP_K

    cnt = counts[0, :n_exp].astype(I32)
    padded = (cnt + rows - 1) // rows * rows
    pad_ends = jnp.cumsum(padded)
    pad_starts = pad_ends - padded
    dest = pad_starts[idx] + rank
    dump = n_assign + jnp.arange(n_rows, dtype=I32) % (2 * rows)
    slot_rows = dump.at[dest.reshape(-1)].set(jnp.arange(n_assign, dtype=I32), unique_indices=True)
    blk_start = jnp.arange(n_blocks, dtype=I32) * rows
    blk_expert = jnp.minimum(jnp.sum((pad_ends[None, :] <= blk_start[:, None]).astype(I32), axis=1), n_exp - 1)
    n_used = (pad_ends[-1:] // rows).astype(I32)

    bgu = (b_gate_up.astype(F32).reshape(n_exp, d_ff // LANES, LANES, 2).transpose(0, 1, 3, 2)
           .reshape(n_exp, 1, 2 * d_ff))
    nch = d // LANES
    src_row = jnp.minimum(slot_rows >> 2, t - 1) * nch
    dst_row = ((slot_rows & (TOP_K - 1)) * tok_pad + (slot_rows >> 2)) * nch
    row_idx = jnp.stack([src_row.reshape(n_blocks, rows), dst_row.reshape(n_blocks, rows)], axis=1)
    ytok = _moe(hn2, row_idx, blk_expert, n_used, w_gate_up, bgu,
                w_down, b_down.reshape(n_exp, 1, d).astype(F32), t, tok_pad)
    return _combine(h, ytok.reshape(TOP_K, tok_pad * (d // LANES), LANES), gates, final_g)


def kernel(x, positions, ln_mix_g, w_in, lam_q1, lam_k1, lam_q2, lam_k2, diff_norm_g, ssm_lam_re, ssm_lam_im,
           ssm_b_re, ssm_b_im, ssm_c_re, ssm_c_im, ssm_d, ssm_log_dt, ssm_w_glu, ssm_b_glu, ssm_norm_g, w_out,
           ln_ffn_g, w_router, b_router, w_gate_up, b_gate_up, w_down, b_down, final_norm_g):
    bsz, s_len, d = x.shape
    depth = w_in.shape[0]
    assert depth == 1
    li = 0
    lambda_init = 0.8 - 0.6 * math.exp(-0.3 * li)
    d_attn = ATTN_HEADS * LANES
    d_ssm = w_in.shape[2] - 3 * d_attn
    t = bsz * s_len

    x2 = x.reshape(t, d)
    q, k, v, u, ut4 = _inproj(x2, positions.reshape(t, 1), ln_mix_g[li], w_in[li], d_attn, d_ssm, bsz)
    vt = v.reshape(bsz, s_len, d_attn).transpose(0, 2, 1)
    a_out = _attention(q.reshape(bsz, s_len, d_attn), k.reshape(bsz, s_len, d_attn), vt,
                       lam_q1[li], lam_k1[li], lam_q2[li], lam_k2[li], diff_norm_g[li], lambda_init)
    yt4 = _s5_core(ut4, ssm_lam_re[li], ssm_lam_im[li], ssm_b_re[li], ssm_b_im[li],
                   ssm_c_re[li], ssm_c_im[li], ssm_log_dt[li])
    h, hn2, idx, gates, rank, counts = _mix(
        x2, a_out.reshape(t, d_attn), yt4, u, ssm_d[li], ssm_w_glu[li], ssm_b_glu[li], ssm_norm_g[li],
        w_out[li], ln_ffn_g[li], w_router[li], b_router[li])
    out = _moe_ffn(h, hn2, idx, gates, rank, counts, w_gate_up[li], b_gate_up[li], w_down[li], b_down[li],
                   final_norm_g)
    return out.reshape(bsz, s_len, d)
```

```python
import functools
import math

import jax
import jax.numpy as jnp
from jax import lax
from jax.experimental import pallas as pl
from jax.experimental.pallas import tpu as pltpu

F32 = jnp.float32
BF16 = jnp.bfloat16
I32 = jnp.int32

RMS_EPS = 1e-6
ATTN_HEADS = 4
ROPE_DIM = 16
ROPE_THETA = 500000.0
SSM_GROUP = 16
SSM_STATE = 64
TOP_K = 4
SWIGLU_LIMIT = 7.0
SWIGLU_ALPHA = 1.702
LANES = 128
SUBLANES = 8
NEG_BIG = -1e30

TOKEN_TILE = 1024
MIX_SUBTILE = 512
ATTN_TILE = 512
ATTN_QUERY_SPLITS = 1
SHIFT_SLACK = 64.0
SSM_BATCH_PAD = SUBLANES
SCAN_BLOCK = 32
MOE_ROWS = 256
VMEM_LIMIT = 56 * 1024 * 1024


def _rms(x, g):
    return x * lax.rsqrt(jnp.mean(x * x, axis=-1, keepdims=True) + RMS_EPS) * g


def _dot_t(a, b, **kw):
    return lax.dot_general(a, b, (((1,), (1,)), ((), ())), preferred_element_type=F32, **kw)


def _inproj_kernel(x_ref, pos_ref, g_ref, w_ref, freq_ref, q_ref, k_ref, v_ref, u_ref, ut_ref, *, d_attn):
    x = x_ref[...]
    n = _rms(x, g_ref[...]).astype(BF16)
    proj = jnp.dot(n, w_ref[...], preferred_element_type=F32)
    ang = pos_ref[...].astype(F32) * freq_ref[...]
    lane = lax.broadcasted_iota(I32, (1, LANES), 1) & (LANES // 2 - 1)
    half = ROPE_DIM // 2
    first = lane < half
    second = jnp.logical_and(lane >= half, lane < ROPE_DIM)
    cos = jnp.cos(ang)
    sin = jnp.sin(ang)
    cm = jnp.where(lane < ROPE_DIM, cos, 1.0)
    sa = jnp.where(first, -sin, 0.0)
    sb = jnp.where(second, sin, 0.0)

    def rope(t):
        return t * cm + pltpu.roll(t, LANES - half, 1) * sa + pltpu.roll(t, half, 1) * sb

    scale = (LANES // 2) ** -0.5 * math.log2(math.e)
    for h in range(d_attn // LANES):
        sl = slice(h * LANES, (h + 1) * LANES)
        q_ref[:, sl] = (rope(proj[:, sl]) * scale).astype(BF16)
        k_ref[:, sl] = rope(proj[:, d_attn + h * LANES:d_attn + (h + 1) * LANES]).astype(BF16)
    v_ref[...] = proj[:, 2 * d_attn:3 * d_attn].astype(BF16)
    u = proj[:, 3 * d_attn:]
    u_ref[...] = u.astype(BF16)
    ut = u.T
    for c in range(ut_ref.shape[1]):
        ut_ref[:, c, :] = ut[:, c * LANES:(c + 1) * LANES]


def _inproj(x2, pos2, g, w_in, d_attn, d_ssm, bsz):
    t, d = x2.shape
    s_len = t // bsz
    tm = min(TOKEN_TILE, s_len)
    tiles = s_len // tm
    cpt = tm // LANES
    assert s_len % tm == 0 and cpt % SUBLANES == 0
    n_in = w_in.shape[1]
    inv_freq = ROPE_THETA ** (-jnp.arange(0, ROPE_DIM, 2, dtype=F32) / ROPE_DIM)
    lane = jnp.arange(LANES) % (LANES // 2)
    freq = jnp.where(lane < ROPE_DIM, inv_freq[lane % (ROPE_DIM // 2)], 0.0).reshape(1, LANES)
    row = lambda i: (i, 0)
    const = lambda i: (0, 0)
    return pl.pallas_call(
        functools.partial(_inproj_kernel, d_attn=d_attn),
        out_shape=(jax.ShapeDtypeStruct((t, d_attn), BF16),) * 3 + (
            jax.ShapeDtypeStruct((t, d_ssm), BF16), jax.ShapeDtypeStruct((bsz, d_ssm, s_len // LANES, LANES), F32)),
        grid=(t // tm,),
        in_specs=[pl.BlockSpec((tm, d), row), pl.BlockSpec((tm, 1), row), pl.BlockSpec((1, d), const),
                  pl.BlockSpec((d, n_in), const), pl.BlockSpec((1, LANES), const)],
        out_specs=(pl.BlockSpec((tm, d_attn), row),) * 3 + (
            pl.BlockSpec((tm, d_ssm), row),
            pl.BlockSpec((None, d_ssm, cpt, LANES), lambda i: (i // tiles, 0, i % tiles, 0))),
        compiler_params=pltpu.CompilerParams(dimension_semantics=("arbitrary",), vmem_limit_bytes=VMEM_LIMIT),
        name="inproj",
    )(x2, pos2, g.reshape(1, d), w_in.astype(BF16), freq)


def _attn_kernel(q_ref, k_ref, vt_ref, lq1_ref, lk1_ref, lq2_ref, lk2_ref, g_ref, o_ref, acc1, acc2, *,
                 tile, lambda_init):
    qi = pl.program_id(2)
    q = q_ref[...]
    lane = lax.broadcasted_iota(I32, (1, LANES), 1)
    zero = jnp.zeros_like(q)
    q_maps = (jnp.where(lane < LANES // 2, q, zero), jnp.where(lane >= LANES // 2, q, zero))
    accs = (acc1, acc2)
    acc1[...] = jnp.zeros_like(acc1)
    acc2[...] = jnp.zeros_like(acc2)
    half = tile // ATTN_QUERY_SPLITS
    units = [(mi, h) for mi in range(2) for h in range(ATTN_QUERY_SPLITS)]
    m0 = jnp.full((1, half), NEG_BIG, F32)
    l0 = jnp.zeros((1, half), F32)

    def scores(j, unit, masked):
        mi, h = unit
        k = k_ref[pl.ds(pl.multiple_of(j * tile, tile), tile), :]
        s = _dot_t(k, q_maps[mi][h * half:(h + 1) * half])
        if masked:
            keep = (lax.broadcasted_iota(I32, (tile, half), 0)
                    <= lax.broadcasted_iota(I32, (tile, half), 1) + h * half)
            s = jnp.where(keep, s, NEG_BIG)
        return s

    def values_t(j):
        return vt_ref[:, pl.ds(pl.multiple_of(j * tile, tile), tile)]

    def rescaling_step(j, carry, masked):
        vt = values_t(j)
        out = []
        all_scores = [scores(j, unit, masked) for unit in units]
        for ui, (mi, h) in enumerate(units):
            m, l = carry[2 * ui], carry[2 * ui + 1]
            cols = slice(h * half, (h + 1) * half)
            s = all_scores[ui]
            m_new = jnp.maximum(m, jnp.max(s, axis=0, keepdims=True))
            alpha = jnp.exp2(m - m_new)
            p = jnp.exp2(s - m_new)
            l_new = alpha * l + jnp.sum(p, axis=0, keepdims=True)
            accs[mi][:, cols] = (accs[mi][:, cols] * alpha
                                 + jnp.dot(vt, p.astype(BF16), preferred_element_type=F32))
            out += [m_new, l_new]
        return tuple(out)

    def step(blocks, carry):
        pend = [None] * (2 * len(units))
        worst = None
        ready = []
        for blk in list(blocks) + [None]:
            fresh = [] if blk is None else [(scores(blk[0], unit, blk[1]), values_t(blk[0])) for unit in units]
            for ui, (s, vt) in enumerate(ready):
                m = carry[2 * ui]
                p = jnp.exp2(s - m)
                over = jnp.max(s, axis=0, keepdims=True) - m
                worst = over if worst is None else jnp.maximum(worst, over)
                new = (jnp.dot(vt, p.astype(BF16), preferred_element_type=F32), jnp.sum(p, axis=0, keepdims=True))
                for t in range(2):
                    pend[2 * ui + t] = new[t] if pend[2 * ui + t] is None else pend[2 * ui + t] + new[t]
            ready = fresh

        def keep_shift():
            out = []
            for ui, (mi, h) in enumerate(units):
                cols = slice(h * half, (h + 1) * half)
                accs[mi][:, cols] = accs[mi][:, cols] + pend[2 * ui]
                out += [carry[2 * ui], carry[2 * ui + 1] + pend[2 * ui + 1]]
            return tuple(out)

        def rescale():
            c = carry
            for j, masked in blocks:
                c = rescaling_step(j, c, masked)
            return c

        return lax.cond(jnp.max(worst) <= SHIFT_SLACK, keep_shift, rescale)

    init = (m0, l0) * len(units)
    carry = lax.cond(qi == 0, lambda: rescaling_step(0, init, True), lambda: rescaling_step(0, init, False))
    plain = jnp.maximum(qi - 1, 0)
    triples = plain // 3
    carry = lax.fori_loop(
        0, triples, lambda t, c: step([(1 + 3 * t, False), (2 + 3 * t, False), (3 + 3 * t, False)], c), carry)
    rest = 1 + 3 * triples
    left = plain - 3 * triples
    carry = lax.cond(left == 2, lambda: step([(rest, False), (rest + 1, False)], carry), lambda: carry)
    carry = lax.cond(left == 1, lambda: step([(rest, False)], carry), lambda: carry)
    carry = lax.cond(qi > 0, lambda: step([(qi, True)], carry), lambda: carry)
    ns = ATTN_QUERY_SPLITS
    l1 = jnp.concatenate([carry[2 * ui + 1] for ui in range(ns)], axis=-1)
    l2 = jnp.concatenate([carry[2 * ui + 1] for ui in range(ns, 2 * ns)], axis=-1)

    lam = (jnp.exp(jnp.sum(lq1_ref[...] * lk1_ref[...], axis=-1, keepdims=True))
           - jnp.exp(jnp.sum(lq2_ref[...] * lk2_ref[...], axis=-1, keepdims=True)) + lambda_init)
    ot = acc1[...] / l1 - lam * (acc2[...] / l2)
    o = _rms(ot.T, g_ref[...]) * (1.0 - lambda_init)
    o_ref[...] = o.astype(o_ref.dtype)


def _attention(q, k, vt, lq1, lk1, lq2, lk2, norm_g, lambda_init):
    b, s, d_attn = q.shape
    heads = d_attn // LANES
    tile = min(ATTN_TILE, s)
    dk = lq1.shape[-1]
    vec = lambda a: a.reshape(1, -1).astype(F32)
    const = lambda bi, h, qi: (0, 0)
    return pl.pallas_call(
        functools.partial(_attn_kernel, tile=tile, lambda_init=lambda_init),
        out_shape=jax.ShapeDtypeStruct((b, s, d_attn), BF16),
        grid=(b, heads, s // tile),
        in_specs=[pl.BlockSpec((None, tile, LANES), lambda bi, h, qi: (bi, qi, h)),
                  pl.BlockSpec((None, s, LANES), lambda bi, h, qi: (bi, 0, h)),
                  pl.BlockSpec((None, LANES, s), lambda bi, h, qi: (bi, h, 0)),
                  pl.BlockSpec((1, dk), const), pl.BlockSpec((1, dk), const),
                  pl.BlockSpec((1, dk), const), pl.BlockSpec((1, dk), const),
                  pl.BlockSpec((1, LANES), const)],
        out_specs=pl.BlockSpec((None, tile, LANES), lambda bi, h, qi: (bi, qi, h)),
        scratch_shapes=[pltpu.VMEM((LANES, tile), F32), pltpu.VMEM((LANES, tile), F32)],
        compiler_params=pltpu.CompilerParams(dimension_semantics=("arbitrary",) * 3,
                                             vmem_limit_bytes=VMEM_LIMIT),
        name="attn",
    )(q, k, vt, vec(lq1), vec(lk1), vec(lq2), vec(lk2), vec(norm_g))


def _ssm_consts(lre, lim, ldt):
    dt = jnp.exp(ldt)
    return lre * dt, lim * dt


def _cpow(e, ldr, ldi):
    mag = jnp.exp(e * ldr)
    ang = e * ldi
    return mag * jnp.cos(ang), mag * jnp.sin(ang)


def _bbar(lre, lim, ldr, ldi, bre, bim):
    mag = jnp.exp(ldr)
    nr = mag * jnp.cos(ldi) - 1.0
    ni = mag * jnp.sin(ldi)
    den = lre * lre + lim * lim
    cr = (nr * lre + ni * lim) / den
    ci = (ni * lre - nr * lim) / den
    return cr * bre - ci * bim, cr * bim + ci * bre


def _crows(pr, pi, xr, xi):
    nc = xr.shape[0]
    rr = jnp.concatenate([pr * xr[c:c + 1] - pi * xi[c:c + 1] for c in range(nc)], axis=0)
    ri = jnp.concatenate([pr * xi[c:c + 1] + pi * xr[c:c + 1] for c in range(nc)], axis=0)
    return rr, ri


def _chunk_rows(ut_ref, gi, uc3):
    bsz, nc2, kc, _ = ut_ref.shape
    nc = nc2 // 2
    for b in range(bsz):
        for c in range(nc):
            uc3[:, b, c * LANES:(c + 1) * LANES] = ut_ref[b, gi * nc + c]
    return uc3[...].reshape(kc * uc3.shape[1], nc * LANES).astype(BF16)


def _s5_local_kernel(ut_ref, lre_ref, lim_ref, ldt_ref, bre_ref, bim_ref, zr_ref, zi_ref, uc3, *, chunk):
    @pl.when(pl.program_id(0) == 0)
    def _():
        uc3[...] = jnp.zeros_like(uc3)

    lane = lax.broadcasted_iota(I32, (1, LANES), 1)
    e = (chunk - 1.0) - lax.broadcasted_iota(I32, (chunk, 1), 0).astype(F32)
    zr = None
    for gi in range(2):
        lre, lim = lre_ref[gi], lim_ref[gi]
        ldr, ldi = _ssm_consts(lre, lim, ldt_ref[gi])
        bbr, bbi = _bbar(lre, lim, ldr, ldi, bre_ref[gi], bim_ref[gi])
        pr, pi = _cpow(e, ldr, ldi)
        wr, wi = _crows(pr, pi, bbr, bbi)
        mine = (lane < LANES // 2) if gi == 0 else (lane >= LANES // 2)
        u = _chunk_rows(ut_ref, gi, uc3)
        pr_ = jnp.dot(u, jnp.where(mine, wr, 0.0).astype(BF16), preferred_element_type=F32)
        pi_ = jnp.dot(u, jnp.where(mine, wi, 0.0).astype(BF16), preferred_element_type=F32)
        zr, zi = (pr_, pi_) if zr is None else (zr + pr_, zi + pi_)
    zr_ref[...] = zr
    zi_ref[...] = zi


def _s5_scan_kernel(zr_ref, zi_ref, lre_ref, lim_ref, ldt_ref, xr_ref, xi_ref, sr, si, ar, ai, *, chunk, nblk):
    @pl.when(pl.program_id(0) == 0)
    def _():
        ldr, ldi = _ssm_consts(lre_ref[...], lim_ref[...], ldt_ref[...])
        mag = jnp.exp(chunk * ldr)
        ar[...] = jnp.broadcast_to(mag * jnp.cos(chunk * ldi), ar.shape)
        ai[...] = jnp.broadcast_to(mag * jnp.sin(chunk * ldi), ai.shape)
        sr[...] = jnp.zeros_like(sr)
        si[...] = jnp.zeros_like(si)

    a_r = ar[...]
    a_i = ai[...]

    def body(kk, carry):
        xr, xi = carry
        rows = pl.ds(pl.multiple_of(kk * SSM_BATCH_PAD, SSM_BATCH_PAD), SSM_BATCH_PAD)
        xr_ref[rows, :] = xr
        xi_ref[rows, :] = xi
        return (a_r * xr - a_i * xi + zr_ref[rows, :], a_r * xi + a_i * xr + zi_ref[rows, :])

    xr, xi = lax.fori_loop(0, nblk, body, (sr[...], si[...]))
    sr[...] = xr
    si[...] = xi


def _s5_out_kernel(ut_ref, xr_ref, xi_ref, lre_ref, lim_ref, ldt_ref, bre_ref, bim_ref, cre_ref, cim_ref,
                   yt_ref, uc3, y3, toep, kern, *, chunk):
    @pl.when(pl.program_id(0) == 0)
    def _():
        uc3[...] = jnp.zeros_like(uc3)

    bsz, nc2, kc, _ = ut_ref.shape
    nc = nc2 // 2
    lane = lax.broadcasted_iota(I32, (1, LANES), 1)
    low = lane < LANES // 2
    step = lax.broadcasted_iota(I32, (chunk, 1), 0).astype(F32)
    causal = lax.broadcasted_iota(I32, (chunk, chunk), 1) >= lax.broadcasted_iota(I32, (chunk, chunk), 0)
    xr = xr_ref[...].astype(BF16)
    xi = xi_ref[...].astype(BF16)
    for gi in range(2):
        lre, lim = lre_ref[gi], lim_ref[gi]
        ldr, ldi = _ssm_consts(lre, lim, ldt_ref[gi])
        bbr, bbi = _bbar(lre, lim, ldr, ldi, bre_ref[gi], bim_ref[gi])
        cre, cim = cre_ref[gi], cim_ref[gi]
        pr, pi = _cpow(step, ldr, ldi)
        gr, gim = _crows(pr, pi, cre, cim)
        kmat = _dot_t(jnp.where(low, bbr, -bbi), jnp.where(low, gr, gim), precision=lax.Precision.HIGHEST)
        for c1 in range(nc):
            kern[c1] = kmat[c1:c1 + 1]

        def toeplitz_rows(c1, _):
            rows = pl.ds(pl.multiple_of(c1 * chunk, chunk), chunk)
            for c in range(nc):
                resp = jnp.broadcast_to(kern[c1, :, c * chunk:(c + 1) * chunk], (chunk, chunk))
                blk = pltpu.roll(resp, 0, 1, stride=1, stride_axis=0)
                toep[rows, c * chunk:(c + 1) * chunk] = jnp.where(causal, blk, 0.0).astype(BF16)
            return 0

        lax.fori_loop(0, nc, toeplitz_rows, 0)
        qr, qi = _cpow(step + 1.0, ldr, ldi)
        vr, vi = _crows(qr, qi, cre, cim)
        mine = low if gi == 0 else jnp.logical_not(low)
        y = jnp.dot(_chunk_rows(ut_ref, gi, uc3), toep[...], preferred_element_type=F32)
        y = y + _dot_t(xr, jnp.where(mine, vr, 0.0).astype(BF16))
        y = y - _dot_t(xi, jnp.where(mine, vi, 0.0).astype(BF16))
        y3[...] = y.reshape(y3.shape)
        for b in range(bsz):
            for c in range(nc):
                yt_ref[b, gi * nc + c] = y3[:, b, c * LANES:(c + 1) * LANES]


def _s5_core(ut4, lam_re, lam_im, b_re, b_im, c_re, c_im, log_dt):
    bsz, _, kc, chunk = ut4.shape
    g, n = lam_re.shape
    nc = SSM_GROUP
    bp = SSM_BATCH_PAD
    rows = kc * bp
    r = nc * chunk
    assert 2 * n == LANES and chunk == LANES and g % 2 == 0 and bsz <= bp and kc % min(SCAN_BLOCK, kc) == 0

    dup = lambda a: jnp.concatenate([a, a], axis=-1).astype(F32)
    lre2 = dup(lam_re)[:, None, :]
    lim2 = dup(lam_im)[:, None, :]
    ldt2 = jnp.broadcast_to(log_dt.astype(F32)[:, None, None], (g, 1, LANES))
    bre2 = dup(b_re.transpose(0, 2, 1))
    bim2 = dup(b_im.transpose(0, 2, 1))
    cre2 = dup(c_re)
    cim2 = dup(c_im)

    pair3 = lambda p: (p, 0, 0)
    vec_spec = pl.BlockSpec((2, 1, LANES), pair3)
    coef_spec = pl.BlockSpec((2, nc, LANES), pair3)
    u_spec = pl.BlockSpec((bsz, 2 * nc, kc, chunk), lambda p: (0, p, 0, 0))
    z_spec = pl.BlockSpec((rows, LANES), lambda p: (0, p))
    params = pltpu.CompilerParams(dimension_semantics=("arbitrary",), vmem_limit_bytes=VMEM_LIMIT)
    rows3 = pltpu.VMEM((kc, bp, r), F32)

    zr, zi = pl.pallas_call(
        functools.partial(_s5_local_kernel, chunk=chunk),
        out_shape=(jax.ShapeDtypeStruct((rows, g * n), F32),) * 2,
        grid=(g // 2,),
        in_specs=[u_spec, vec_spec, vec_spec, vec_spec, coef_spec, coef_spec],
        out_specs=(z_spec, z_spec),
        scratch_shapes=[rows3],
        compiler_params=params, name="s5_local",
    )(ut4, lre2, lim2, ldt2, bre2, bim2)

    nblk = min(SCAN_BLOCK, kc)
    flat = lambda a: a.astype(F32).reshape(1, g * n)
    ldt_flat = jnp.broadcast_to(log_dt.astype(F32)[:, None], (g, n)).reshape(1, g * n)
    blk = pl.BlockSpec((nblk * bp, g * n), lambda i: (i, 0))
    cst = pl.BlockSpec((1, g * n), lambda i: (0, 0))
    xr, xi = pl.pallas_call(
        functools.partial(_s5_scan_kernel, chunk=chunk, nblk=nblk),
        out_shape=(jax.ShapeDtypeStruct((rows, g * n), F32),) * 2,
        grid=(kc // nblk,),
        in_specs=[blk, blk, cst, cst, cst],
        out_specs=(blk, blk),
        scratch_shapes=[pltpu.VMEM((bp, g * n), F32)] * 4,
        compiler_params=params, name="s5_scan",
    )(zr, zi, flat(lam_re), flat(lam_im), ldt_flat)

    return pl.pallas_call(
        functools.partial(_s5_out_kernel, chunk=chunk),
        out_shape=jax.ShapeDtypeStruct(ut4.shape, F32),
        grid=(g // 2,),
        in_specs=[u_spec, z_spec, z_spec, vec_spec, vec_spec, vec_spec, coef_spec, coef_spec, coef_spec, coef_spec],
        out_specs=u_spec,
        scratch_shapes=[rows3, rows3, pltpu.VMEM((r, r), BF16), pltpu.VMEM((nc, 1, r), F32)],
        compiler_params=params, name="s5_out",
    )(ut4, xr, xi, lre2, lim2, ldt2, bre2, bim2, cre2, cim2)


def _mix_kernel(x_ref, a_ref, y_ref, u_ref, d_ref, wglu_ref, bglu_ref, gs_ref, wout_ref, gffn_ref,
                wr_ref, wrlo_ref, br_ref, h_ref, hn_ref, idx_ref, gate_ref, rank_ref, cnt_ref, run_ref, *, d_attn):
    i = pl.program_id(0)

    @pl.when(i == 0)
    def _():
        run_ref[...] = jnp.zeros_like(run_ref)

    tm = min(MIX_SUBTILE, x_ref.shape[0])
    nch = x_ref.shape[1] // LANES
    for sub in range(x_ref.shape[0] // tm):
        _mix_subtile(sub, tm, nch, d_attn, x_ref, a_ref, y_ref, u_ref, d_ref, wglu_ref, bglu_ref, gs_ref,
                     wout_ref, gffn_ref, wr_ref, wrlo_ref, br_ref, h_ref, hn_ref, idx_ref, gate_ref, rank_ref, run_ref)
    cnt_ref[...] = run_ref[...]


def _mix_subtile(sub, tm, nch, d_attn, x_ref, a_ref, y_ref, u_ref, d_ref, wglu_ref, bglu_ref, gs_ref, wout_ref,
                 gffn_ref, wr_ref, wrlo_ref, br_ref, h_ref, hn_ref, idx_ref, gate_ref, rank_ref, run_ref):
    rows = pl.ds(sub * tm, tm)
    cps = tm // LANES
    yt = jnp.concatenate([y_ref[:, sub * cps + c, :] for c in range(cps)], axis=-1)
    y = yt.T + d_ref[...] * u_ref[rows, :].astype(F32)
    y = 0.5 * y * (1.0 + lax.erf(y * (0.5 ** 0.5)))
    z = jnp.dot(y.astype(BF16), wglu_ref[...], preferred_element_type=F32) + bglu_ref[...]
    y = y * jax.nn.sigmoid(z)
    s_out = _rms(y, gs_ref[...])
    mix = (jnp.dot(a_ref[rows, :], wout_ref[:d_attn, :], preferred_element_type=F32)
           + jnp.dot(s_out.astype(BF16), wout_ref[d_attn:, :], preferred_element_type=F32))
    h = x_ref[rows, :] + mix
    h_ref[rows, :] = h
    hn = _rms(h, gffn_ref[...])
    for c in range(nch):
        hn_ref[pl.ds(sub * tm * nch + c, tm, stride=nch), :] = hn[:, c * LANES:(c + 1) * LANES]

    hn_hi = hn.astype(BF16)
    hn_lo = (hn - hn_hi.astype(F32)).astype(BF16)
    lg = (jnp.dot(hn_hi, wr_ref[...], preferred_element_type=F32)
          + jnp.dot(hn_hi, wrlo_ref[...], preferred_element_type=F32)
          + jnp.dot(hn_lo, wr_ref[...], preferred_element_type=F32)) + br_ref[...]
    lane = lax.broadcasted_iota(I32, (tm, LANES), 1)
    lane_f = lane.astype(F32)
    vals, hots = [], []
    idx_all = jnp.zeros((tm, LANES), F32)
    for kk in range(TOP_K):
        mx = jnp.max(lg, axis=-1, keepdims=True)
        ix = jnp.min(jnp.where(lg == mx, lane_f, float(LANES)), axis=-1, keepdims=True)
        hot = lane_f == ix
        vals.append(mx)
        hots.append(hot)
        idx_all = jnp.where(lane == kk, ix, idx_all)
        lg = jnp.where(hot, -3e38, lg)
    ex = [jnp.exp(v - vals[0]) for v in vals]
    den = ex[0] + ex[1] + ex[2] + ex[3]
    gate_all = jnp.zeros((tm, LANES), F32)
    for kk in range(TOP_K):
        gate_all = jnp.where(lane == kk, ex[kk] / den, gate_all)

    picked = jnp.zeros((tm, LANES), F32)
    for hot in hots:
        picked = jnp.where(hot, 1.0, picked)
    tri = (lax.broadcasted_iota(I32, (tm, tm), 1) < lax.broadcasted_iota(I32, (tm, tm), 0))
    before = jnp.dot(jnp.where(tri, 1.0, 0.0).astype(BF16), picked.astype(BF16), preferred_element_type=F32)
    before = before + run_ref[...]
    rank_all = jnp.zeros((tm, LANES), F32)
    for kk, hot in enumerate(hots):
        rk = jnp.sum(jnp.where(hot, before, 0.0), axis=-1, keepdims=True)
        rank_all = jnp.where(lane == kk, rk, rank_all)
    run_ref[...] = run_ref[...] + jnp.sum(picked, axis=0, keepdims=True)
    idx_ref[rows, :] = idx_all[:, :TOP_K].astype(I32)
    gate_ref[rows, :] = gate_all[:, :TOP_K]
    rank_ref[rows, :] = rank_all[:, :TOP_K].astype(I32)


def _mix(x2, a_out, yt4, u, ssm_d, w_glu, b_glu, ssm_norm_g, w_out, ln_ffn_g, w_router, b_router):
    t, d = x2.shape
    d_attn = a_out.shape[1]
    bsz, d_ssm, kc, _ = yt4.shape
    s_len = t // bsz
    n_exp = w_router.shape[1]
    tm = min(TOKEN_TILE, s_len)
    tiles = s_len // tm
    nch = d // LANES
    wr = jnp.zeros((d, LANES), F32).at[:, :n_exp].set(w_router.astype(F32))
    br = jnp.full((1, LANES), NEG_BIG, F32).at[0, :n_exp].set(b_router.astype(F32))
    wr_hi = wr.astype(BF16)
    wr_lo = (wr - wr_hi.astype(F32)).astype(BF16)
    row = lambda i: (i, 0)
    const = lambda i: (0, 0)
    vec = lambda a: a.reshape(1, -1).astype(F32)
    return pl.pallas_call(
        functools.partial(_mix_kernel, d_attn=d_attn),
        out_shape=(jax.ShapeDtypeStruct((t, d), F32), jax.ShapeDtypeStruct((t * nch, LANES), F32),
                   jax.ShapeDtypeStruct((t, TOP_K), I32), jax.ShapeDtypeStruct((t, TOP_K), F32),
                   jax.ShapeDtypeStruct((t, TOP_K), I32), jax.ShapeDtypeStruct((1, LANES), F32)),
        grid=(t // tm,),
        in_specs=[pl.BlockSpec((tm, d), row), pl.BlockSpec((tm, d_attn), row),
                  pl.BlockSpec((None, d_ssm, tm // LANES, LANES), lambda i: (i // tiles, 0, i % tiles, 0)),
                  pl.BlockSpec((tm, d_ssm), row), pl.BlockSpec((1, d_ssm), const),
                  pl.BlockSpec((d_ssm, d_ssm), const), pl.BlockSpec((1, d_ssm), const),
                  pl.BlockSpec((1, d_ssm), const), pl.BlockSpec((d_attn + d_ssm, d), const),
                  pl.BlockSpec((1, d), const), pl.BlockSpec((d, LANES), const), pl.BlockSpec((d, LANES), const),
                  pl.BlockSpec((1, LANES), const)],
        out_specs=(pl.BlockSpec((tm, d), row), pl.BlockSpec((tm * nch, LANES), row),
                   pl.BlockSpec((tm, TOP_K), row), pl.BlockSpec((tm, TOP_K), row), pl.BlockSpec((tm, TOP_K), row),
                   pl.BlockSpec((1, LANES), const)),
        scratch_shapes=[pltpu.VMEM((1, LANES), F32)],
        compiler_params=pltpu.CompilerParams(dimension_semantics=("arbitrary",), vmem_limit_bytes=VMEM_LIMIT),
        name="mix",
    )(x2, a_out, yt4, u, vec(ssm_d), w_glu.astype(BF16), vec(b_glu), vec(ssm_norm_g), w_out.astype(BF16),
      vec(ln_ffn_g), wr_hi, wr_lo, br)


IDX_RING = 4


def _moe_kernel(blk_e_ref, nused_ref, slot_hbm, hn_hbm, wgu_ref, bgu_ref, wd_ref, bd_ref, ytok_hbm,
                idx_smem, xbuf, ybuf, wperm, wdown, idx_sem, g_sem, s_sem, *,
                n_tok, n_blocks, tok_pad, rows, nch):
    i = pl.program_id(0)
    n_used = nused_ref[0]
    d_ff = wdown.shape[0]
    grp = 2 * LANES

    def idx_copy(blk, src=None):
        src = jnp.minimum(blk, n_blocks - 1) if src is None else src
        slot = blk & (IDX_RING - 1)
        return pltpu.make_async_copy(slot_hbm.at[src], idx_smem.at[slot], idx_sem.at[slot])

    blk_rows = rows * nch

    def half(par):
        return pl.ds(par * blk_rows, blk_rows)

    def gather_wait(par):
        pltpu.make_async_copy(hn_hbm.at[pl.ds(0, blk_rows)], xbuf.at[half(par)], g_sem.at[par]).wait()

    def scatter_wait(par):
        pltpu.make_async_copy(ybuf.at[half(par)], ytok_hbm.at[pl.ds(0, blk_rows)], s_sem.at[par]).wait()

    def issue_gather(blk, par):
        slot = blk & (IDX_RING - 1)
        for r in range(rows):
            src = pl.ds(pl.multiple_of(idx_smem[slot, 0, r], nch), nch)
            pltpu.make_async_copy(hn_hbm.at[src], xbuf.at[pl.ds(par * blk_rows + r * nch, nch)],
                                  g_sem.at[par]).start(priority=r % 2)

    def issue_scatter(blk, par):
        slot = blk & (IDX_RING - 1)
        for r in range(rows):
            dst = pl.ds(pl.multiple_of(idx_smem[slot, 1, r], nch), nch)
            pltpu.make_async_copy(ybuf.at[pl.ds(par * blk_rows + r * nch, nch)], ytok_hbm.at[dst],
                                  s_sem.at[par]).start(priority=r % 2)

    @pl.when(i == 0)
    def _():
        idx_copy(0).start()
        idx_copy(1).start()
        idx_copy(-1, src=n_blocks - 1).start()
        spare = (tok_pad - n_tok) * nch
        ybuf[...] = jnp.zeros_like(ybuf)
        fills = [pltpu.make_async_copy(ybuf.at[pl.ds(0, spare)],
                                       ytok_hbm.at[pl.ds((kk * tok_pad + n_tok) * nch, spare)], s_sem.at[0])
                 for kk in range(TOP_K)]
        for f in fills:
            f.start()
        for f in fills:
            f.wait()
        idx_copy(0).wait()
        idx_copy(-1, src=n_blocks - 1).wait()
        issue_gather(0, 0)

    def block(par):
        idx_copy(i + 1).wait()
        gather_wait(par)

        @pl.when(i > 0)
        def _():
            scatter_wait(par)

        expert = blk_e_ref[i]

        @pl.when(jnp.logical_or(i == 0, expert != blk_e_ref[jnp.maximum(i - 1, 0)]))
        def _():
            src = lax.broadcasted_iota(I32, (grp, grp), 0)
            col = lax.broadcasted_iota(I32, (grp, grp), 1)
            want = jnp.where(col < LANES, 2 * col, 2 * (col - LANES) + 1)
            perm = jnp.where(src == want, 1.0, 0.0).astype(BF16)
            for gi in range(wperm.shape[1] // grp):
                cols = slice(gi * grp, (gi + 1) * grp)
                wperm[:, cols] = jnp.dot(wgu_ref[:, cols].astype(BF16), perm,
                                         preferred_element_type=F32).astype(BF16)
            wdown[...] = wd_ref[...].astype(BF16)

        issue_gather(i + 1, 1 - par)
        issue_scatter(i - 1, 1 - par)
        x = jnp.concatenate([xbuf[pl.ds(par * blk_rows + c, rows, stride=nch), :] for c in range(nch)],
                            axis=-1).astype(BF16)
        gu = jnp.dot(x, wperm[...], preferred_element_type=F32) + bgu_ref[...]
        acts = []
        for gi in range(d_ff // LANES):
            gate = jnp.minimum(gu[:, gi * grp:gi * grp + LANES], SWIGLU_LIMIT)
            up = jnp.clip(gu[:, gi * grp + LANES:(gi + 1) * grp], -SWIGLU_LIMIT, SWIGLU_LIMIT)
            acts.append((gate * jax.nn.sigmoid(SWIGLU_ALPHA * gate) * (up + 1.0)).astype(BF16))
        y = jnp.dot(jnp.concatenate(acts, axis=-1), wdown[...], preferred_element_type=F32) + bd_ref[...]
        for c in range(nch):
            ybuf[pl.ds(par * blk_rows + c, rows, stride=nch), :] = y[:, c * LANES:(c + 1) * LANES]
        idx_copy(i + 2).start()

        @pl.when(i == n_used - 1)
        def _():
            issue_scatter(i, par)
            scatter_wait(1 - par)
            scatter_wait(par)
            gather_wait(1 - par)
            idx_copy(i + 2).wait()

    for par in range(2):
        pl.when(jnp.logical_and(i < n_used, (i & 1) == par))(functools.partial(block, par))


def _moe(hn2, slot_rows, blk_expert, n_used, wgu, bgu, wd, bd, n_tok, tok_pad):
    n_blocks, _, rows = slot_rows.shape
    n_exp, d, d_ff2 = wgu.shape
    d_ff = d_ff2 // 2
    nch = d // LANES
    wmap = lambda i, be, nu: (be[i], 0, 0)
    return pl.pallas_call(
        functools.partial(_moe_kernel, n_tok=n_tok, n_blocks=n_blocks, tok_pad=tok_pad, rows=rows, nch=nch),
        out_shape=jax.ShapeDtypeStruct((TOP_K * tok_pad * nch, LANES), F32),
        grid_spec=pltpu.PrefetchScalarGridSpec(
            num_scalar_prefetch=2,
            grid=(n_blocks,),
            in_specs=[pl.BlockSpec(memory_space=pl.ANY), pl.BlockSpec(memory_space=pl.ANY),
                      pl.BlockSpec((None, d, d_ff2), wmap), pl.BlockSpec((None, 1, d_ff2), wmap),
                      pl.BlockSpec((None, d_ff, d), wmap), pl.BlockSpec((None, 1, d), wmap)],
            out_specs=pl.BlockSpec(memory_space=pl.ANY),
            scratch_shapes=[pltpu.SMEM((IDX_RING, 2, rows), I32), pltpu.VMEM((2 * rows * nch, LANES), F32),
                            pltpu.VMEM((2 * rows * nch, LANES), F32), pltpu.VMEM((d, d_ff2), BF16),
                            pltpu.VMEM((d_ff, d), BF16), pltpu.SemaphoreType.DMA((IDX_RING,)),
                            pltpu.SemaphoreType.DMA((2,)), pltpu.SemaphoreType.DMA((2,))]),
        compiler_params=pltpu.CompilerParams(dimension_semantics=("arbitrary",), vmem_limit_bytes=VMEM_LIMIT),
        name="moe",
    )(blk_expert, n_used, slot_rows, hn2, wgu, bgu, wd, bd)


def _combine_kernel(h_ref, y_ref, gate_ref, g_ref, o_ref):
    tm, d = h_ref.shape
    nch = d // LANES
    gates = gate_ref[...]
    parts = []
    ss = None
    for c in range(nch):
        acc = h_ref[:, c * LANES:(c + 1) * LANES]
        for kk in range(TOP_K):
            acc = acc + gates[:, kk:kk + 1] * y_ref[kk, pl.ds(c, tm, stride=nch), :]
        parts.append(acc)
        sq = jnp.sum(acc * acc, axis=-1, keepdims=True)
        ss = sq if ss is None else ss + sq
    inv = lax.rsqrt(ss / (nch * LANES) + RMS_EPS)
    for c in range(nch):
        o_ref[:, c * LANES:(c + 1) * LANES] = parts[c] * inv * g_ref[:, c * LANES:(c + 1) * LANES]


def _combine(h, ytok3, gates, final_g):
    t, d = h.shape
    nch = d // LANES
    tm = min(TOKEN_TILE // 2, t)
    return pl.pallas_call(
        _combine_kernel,
        out_shape=jax.ShapeDtypeStruct((t, d), F32),
        grid=(t // tm,),
        in_specs=[pl.BlockSpec((tm, d), lambda i: (i, 0)),
                  pl.BlockSpec((TOP_K, tm * nch, LANES), lambda i: (0, i, 0)),
                  pl.BlockSpec((tm, TOP_K), lambda i: (i, 0)), pl.BlockSpec((1, d), lambda i: (0, 0))],
        out_specs=pl.BlockSpec((tm, d), lambda i: (i, 0)),
        compiler_params=pltpu.CompilerParams(dimension_semantics=("arbitrary",), vmem_limit_bytes=VMEM_LIMIT),
        name="combine",
    )(h, ytok3, gates, final_g.reshape(1, d).astype(F32))


def _moe_ffn(h, hn2, idx, gates, rank, counts, w_gate_up, b_gate_up, w_down, b_down, final_g):
    t, d = h.shape
    n_exp = w_gate_up.shape[0]
    d_ff = w_down.shape[1]
    rows = MOE_ROWS
    n_assign = t * TOP_K
    n_rows = n_assign + n_exp * rows
    n_blocks = n_rows // rows
    tok_pad = t + (2 * rows) // TOP_K

    cnt = counts[0, :n_exp].astype(I32)
    padded = (cnt + rows - 1) // rows * rows
    pad_ends = jnp.cumsum(padded)
    pad_starts = pad_ends - padded
    dest = pad_starts[idx] + rank
    dump = n_assign + jnp.arange(n_rows, dtype=I32) % (2 * rows)
    slot_rows = dump.at[dest.reshape(-1)].set(jnp.arange(n_assign, dtype=I32), unique_indices=True)
    blk_start = jnp.arange(n_blocks, dtype=I32) * rows
    blk_expert = jnp.minimum(jnp.sum((pad_ends[None, :] <= blk_start[:, None]).astype(I32), axis=1), n_exp - 1)
    n_used = (pad_ends[-1:] // rows).astype(I32)

    bgu = (b_gate_up.astype(F32).reshape(n_exp, d_ff // LANES, LANES, 2).transpose(0, 1, 3, 2)
           .reshape(n_exp, 1, 2 * d_ff))
    nch = d // LANES
    src_row = jnp.minimum(slot_rows >> 2, t - 1) * nch
    dst_row = ((slot_rows & (TOP_K - 1)) * tok_pad + (slot_rows >> 2)) * nch
    row_idx = jnp.stack([src_row.reshape(n_blocks, rows), dst_row.reshape(n_blocks, rows)], axis=1)
    ytok = _moe(hn2, row_idx, blk_expert, n_used, w_gate_up, bgu,
                w_down, b_down.reshape(n_exp, 1, d).astype(F32), t, tok_pad)
    return _combine(h, ytok.reshape(TOP_K, tok_pad * (d // LANES), LANES), gates, final_g)


def kernel(x, positions, ln_mix_g, w_in, lam_q1, lam_k1, lam_q2, lam_k2, diff_norm_g, ssm_lam_re, ssm_lam_im,
           ssm_b_re, ssm_b_im, ssm_c_re, ssm_c_im, ssm_d, ssm_log_dt, ssm_w_glu, ssm_b_glu, ssm_norm_g, w_out,
           ln_ffn_g, w_router, b_router, w_gate_up, b_gate_up, w_down, b_down, final_norm_g):
    bsz, s_len, d = x.shape
    depth = w_in.shape[0]
    assert depth == 1
    li = 0
    lambda_init = 0.8 - 0.6 * math.exp(-0.3 * li)
    d_attn = ATTN_HEADS * LANES
    d_ssm = w_in.shape[2] - 3 * d_attn
    t = bsz * s_len

    x2 = x.reshape(t, d)
    q, k, v, u, ut4 = _inproj(x2, positions.reshape(t, 1), ln_mix_g[li], w_in[li], d_attn, d_ssm, bsz)
    vt = v.reshape(bsz, s_len, d_attn).transpose(0, 2, 1)
    a_out = _attention(q.reshape(bsz, s_len, d_attn), k.reshape(bsz, s_len, d_attn), vt,
                       lam_q1[li], lam_k1[li], lam_q2[li], lam_k2[li], diff_norm_g[li], lambda_init)
    yt4 = _s5_core(ut4, ssm_lam_re[li], ssm_lam_im[li], ssm_b_re[li], ssm_b_im[li],
                   ssm_c_re[li], ssm_c_im[li], ssm_log_dt[li])
    h, hn2, idx, gates, rank, counts = _mix(
        x2, a_out.reshape(t, d_attn), yt4, u, ssm_d[li], ssm_w_glu[li], ssm_b_glu[li], ssm_norm_g[li],
        w_out[li], ln_ffn_g[li], w_router[li], b_router[li])
    out = _moe_ffn(h, hn2, idx, gates, rank, counts, w_gate_up[li], b_gate_up[li], w_down[li], b_down[li],
                   final_norm_g)
    return out.reshape(bsz, s_len, d)
```

```python
import functools
import math

import jax
import jax.numpy as jnp
from jax import lax
from jax.experimental import pallas as pl
from jax.experimental.pallas import tpu as pltpu

F32 = jnp.float32
BF16 = jnp.bfloat16
I32 = jnp.int32

RMS_EPS = 1e-6
ATTN_HEADS = 4
ROPE_DIM = 16
ROPE_THETA = 500000.0
SSM_GROUP = 16
SSM_STATE = 64
TOP_K = 4
SWIGLU_LIMIT = 7.0
SWIGLU_ALPHA = 1.702
LANES = 128
SUBLANES = 8
NEG_BIG = -1e30

TOKEN_TILE = 1024
MIX_SUBTILE = 512
ATTN_TILE = 512
ATTN_QUERY_SPLITS = 1
SHIFT_SLACK = 64.0
SSM_BATCH_PAD = SUBLANES
SCAN_BLOCK = 32
MOE_ROWS = 256
VMEM_LIMIT = 56 * 1024 * 1024


def _rms(x, g):
    return x * lax.rsqrt(jnp.mean(x * x, axis=-1, keepdims=True) + RMS_EPS) * g


def _dot_t(a, b, **kw):
    return lax.dot_general(a, b, (((1,), (1,)), ((), ())), preferred_element_type=F32, **kw)


def _dot_t_split(a, b):
    a_hi, b_hi = a.astype(BF16), b.astype(BF16)
    a_lo = (a - a_hi.astype(F32)).astype(BF16)
    b_lo = (b - b_hi.astype(F32)).astype(BF16)
    return _dot_t(a_hi, b_hi) + _dot_t(a_hi, b_lo) + _dot_t(a_lo, b_hi)


def _inproj_kernel(x_ref, pos_ref, g_ref, w_ref, freq_ref, q_ref, k_ref, v_ref, u_ref, ut_ref, *, d_attn):
    x = x_ref[...]
    n = _rms(x, g_ref[...]).astype(BF16)
    proj = jnp.dot(n, w_ref[...], preferred_element_type=F32)
    ang = pos_ref[...].astype(F32) * freq_ref[...]
    lane = lax.broadcasted_iota(I32, (1, LANES), 1) & (LANES // 2 - 1)
    half = ROPE_DIM // 2
    first = lane < half
    second = jnp.logical_and(lane >= half, lane < ROPE_DIM)
    cos = jnp.cos(ang)
    sin = jnp.sin(ang)
    cm = jnp.where(lane < ROPE_DIM, cos, 1.0)
    sa = jnp.where(first, -sin, 0.0)
    sb = jnp.where(second, sin, 0.0)

    def rope(t):
        return t * cm + pltpu.roll(t, LANES - half, 1) * sa + pltpu.roll(t, half, 1) * sb

    scale = (LANES // 2) ** -0.5 * math.log2(math.e)
    for h in range(d_attn // LANES):
        sl = slice(h * LANES, (h + 1) * LANES)
        q_ref[:, sl] = (rope(proj[:, sl]) * scale).astype(BF16)
        k_ref[:, sl] = rope(proj[:, d_attn + h * LANES:d_attn + (h + 1) * LANES]).astype(BF16)
    v_ref[...] = proj[:, 2 * d_attn:3 * d_attn].astype(BF16)
    u = proj[:, 3 * d_attn:]
    u_ref[...] = u.astype(BF16)
    ut = u.T
    for c in range(ut_ref.shape[1]):
        ut_ref[:, c, :] = ut[:, c * LANES:(c + 1) * LANES]


def _inproj(x2, pos2, g, w_in, d_attn, d_ssm, bsz):
    t, d = x2.shape
    s_len = t // bsz
    tm = min(TOKEN_TILE, s_len)
    tiles = s_len // tm
    cpt = tm // LANES
    assert s_len % tm == 0 and cpt % SUBLANES == 0
    n_in = w_in.shape[1]
    inv_freq = ROPE_THETA ** (-jnp.arange(0, ROPE_DIM, 2, dtype=F32) / ROPE_DIM)
    lane = jnp.arange(LANES) % (LANES // 2)
    freq = jnp.where(lane < ROPE_DIM, inv_freq[lane % (ROPE_DIM // 2)], 0.0).reshape(1, LANES)
    row = lambda i: (i, 0)
    const = lambda i: (0, 0)
    return pl.pallas_call(
        functools.partial(_inproj_kernel, d_attn=d_attn),
        out_shape=(jax.ShapeDtypeStruct((t, d_attn), BF16),) * 3 + (
            jax.ShapeDtypeStruct((t, d_ssm), BF16), jax.ShapeDtypeStruct((bsz, d_ssm, s_len // LANES, LANES), F32)),
        grid=(t // tm,),
        in_specs=[pl.BlockSpec((tm, d), row), pl.BlockSpec((tm, 1), row), pl.BlockSpec((1, d), const),
                  pl.BlockSpec((d, n_in), const), pl.BlockSpec((1, LANES), const)],
        out_specs=(pl.BlockSpec((tm, d_attn), row),) * 3 + (
            pl.BlockSpec((tm, d_ssm), row),
            pl.BlockSpec((None, d_ssm, cpt, LANES), lambda i: (i // tiles, 0, i % tiles, 0))),
        compiler_params=pltpu.CompilerParams(dimension_semantics=("arbitrary",), vmem_limit_bytes=VMEM_LIMIT),
        name="inproj",
    )(x2, pos2, g.reshape(1, d), w_in.astype(BF16), freq)


def _attn_kernel(q_ref, k_ref, vt_ref, lq1_ref, lk1_ref, lq2_ref, lk2_ref, g_ref, o_ref, acc1, acc2, *,
                 tile, lambda_init):
    qi = pl.program_id(2)
    q = q_ref[...]
    lane = lax.broadcasted_iota(I32, (1, LANES), 1)
    zero = jnp.zeros_like(q)
    q_maps = (jnp.where(lane < LANES // 2, q, zero), jnp.where(lane >= LANES // 2, q, zero))
    accs = (acc1, acc2)
    acc1[...] = jnp.zeros_like(acc1)
    acc2[...] = jnp.zeros_like(acc2)
    half = tile // ATTN_QUERY_SPLITS
    units = [(mi, h) for mi in range(2) for h in range(ATTN_QUERY_SPLITS)]
    m0 = jnp.full((1, half), NEG_BIG, F32)
    l0 = jnp.zeros((1, half), F32)

    def scores(j, unit, masked):
        mi, h = unit
        k = k_ref[pl.ds(pl.multiple_of(j * tile, tile), tile), :]
        s = _dot_t(k, q_maps[mi][h * half:(h + 1) * half])
        if masked:
            keep = (lax.broadcasted_iota(I32, (tile, half), 0)
                    <= lax.broadcasted_iota(I32, (tile, half), 1) + h * half)
            s = jnp.where(keep, s, NEG_BIG)
        return s

    def values_t(j):
        return vt_ref[:, pl.ds(pl.multiple_of(j * tile, tile), tile)]

    def rescaling_step(j, carry, masked):
        vt = values_t(j)
        out = []
        all_scores = [scores(j, unit, masked) for unit in units]
        for ui, (mi, h) in enumerate(units):
            m, l = carry[2 * ui], carry[2 * ui + 1]
            cols = slice(h * half, (h + 1) * half)
            s = all_scores[ui]
            m_new = jnp.maximum(m, jnp.max(s, axis=0, keepdims=True))
            alpha = jnp.exp2(m - m_new)
            p = jnp.exp2(s - m_new)
            l_new = alpha * l + jnp.sum(p, axis=0, keepdims=True)
            accs[mi][:, cols] = (accs[mi][:, cols] * alpha
                                 + jnp.dot(vt, p.astype(BF16), preferred_element_type=F32))
            out += [m_new, l_new]
        return tuple(out)

    def step(blocks, carry):
        pend = [None] * (2 * len(units))
        worst = None
        ready = []
        for blk in list(blocks) + [None]:
            fresh = [] if blk is None else [(scores(blk[0], unit, blk[1]), values_t(blk[0])) for unit in units]
            for ui, (s, vt) in enumerate(ready):
                m = carry[2 * ui]
                p = jnp.exp2(s - m)
                over = jnp.max(s, axis=0, keepdims=True) - m
                worst = over if worst is None else jnp.maximum(worst, over)
                new = (jnp.dot(vt, p.astype(BF16), preferred_element_type=F32), jnp.sum(p, axis=0, keepdims=True))
                for t in range(2):
                    pend[2 * ui + t] = new[t] if pend[2 * ui + t] is None else pend[2 * ui + t] + new[t]
            ready = fresh

        def keep_shift():
            out = []
            for ui, (mi, h) in enumerate(units):
                cols = slice(h * half, (h + 1) * half)
                accs[mi][:, cols] = accs[mi][:, cols] + pend[2 * ui]
                out += [carry[2 * ui], carry[2 * ui + 1] + pend[2 * ui + 1]]
            return tuple(out)

        def rescale():
            c = carry
            for j, masked in blocks:
                c = rescaling_step(j, c, masked)
            return c

        return lax.cond(jnp.max(worst) <= SHIFT_SLACK, keep_shift, rescale)

    init = (m0, l0) * len(units)
    carry = lax.cond(qi == 0, lambda: rescaling_step(0, init, True), lambda: rescaling_step(0, init, False))
    plain = jnp.maximum(qi - 1, 0)
    triples = plain // 3
    carry = lax.fori_loop(
        0, triples, lambda t, c: step([(1 + 3 * t, False), (2 + 3 * t, False), (3 + 3 * t, False)], c), carry)
    rest = 1 + 3 * triples
    left = plain - 3 * triples
    carry = lax.cond(left == 2, lambda: step([(rest, False), (rest + 1, False)], carry), lambda: carry)
    carry = lax.cond(left == 1, lambda: step([(rest, False)], carry), lambda: carry)
    carry = lax.cond(qi > 0, lambda: step([(qi, True)], carry), lambda: carry)
    ns = ATTN_QUERY_SPLITS
    l1 = jnp.concatenate([carry[2 * ui + 1] for ui in range(ns)], axis=-1)
    l2 = jnp.concatenate([carry[2 * ui + 1] for ui in range(ns, 2 * ns)], axis=-1)

    lam = (jnp.exp(jnp.sum(lq1_ref[...] * lk1_ref[...], axis=-1, keepdims=True))
           - jnp.exp(jnp.sum(lq2_ref[...] * lk2_ref[...], axis=-1, keepdims=True)) + lambda_init)
    ot = acc1[...] / l1 - lam * (acc2[...] / l2)
    o = _rms(ot.T, g_ref[...]) * (1.0 - lambda_init)
    o_ref[...] = o.astype(o_ref.dtype)


def _attention(q, k, vt, lq1, lk1, lq2, lk2, norm_g, lambda_init):
    b, s, d_attn = q.shape
    heads = d_attn // LANES
    tile = min(ATTN_TILE, s)
    dk = lq1.shape[-1]
    vec = lambda a: a.reshape(1, -1).astype(F32)
    const = lambda bi, h, qi: (0, 0)
    return pl.pallas_call(
        functools.partial(_attn_kernel, tile=tile, lambda_init=lambda_init),
        out_shape=jax.ShapeDtypeStruct((b, s, d_attn), BF16),
        grid=(b, heads, s // tile),
        in_specs=[pl.BlockSpec((None, tile, LANES), lambda bi, h, qi: (bi, qi, h)),
                  pl.BlockSpec((None, s, LANES), lambda bi, h, qi: (bi, 0, h)),
                  pl.BlockSpec((None, LANES, s), lambda bi, h, qi: (bi, h, 0)),
                  pl.BlockSpec((1, dk), const), pl.BlockSpec((1, dk), const),
                  pl.BlockSpec((1, dk), const), pl.BlockSpec((1, dk), const),
                  pl.BlockSpec((1, LANES), const)],
        out_specs=pl.BlockSpec((None, tile, LANES), lambda bi, h, qi: (bi, qi, h)),
        scratch_shapes=[pltpu.VMEM((LANES, tile), F32), pltpu.VMEM((LANES, tile), F32)],
        compiler_params=pltpu.CompilerParams(dimension_semantics=("arbitrary",) * 3,
                                             vmem_limit_bytes=VMEM_LIMIT),
        name="attn",
    )(q, k, vt, vec(lq1), vec(lk1), vec(lq2), vec(lk2), vec(norm_g))


def _ssm_consts(lre, lim, ldt):
    dt = jnp.exp(ldt)
    return lre * dt, lim * dt


def _cpow(e, ldr, ldi):
    mag = jnp.exp(e * ldr)
    ang = e * ldi
    return mag * jnp.cos(ang), mag * jnp.sin(ang)


def _bbar(lre, lim, ldr, ldi, bre, bim):
    mag = jnp.exp(ldr)
    nr = mag * jnp.cos(ldi) - 1.0
    ni = mag * jnp.sin(ldi)
    den = lre * lre + lim * lim
    cr = (nr * lre + ni * lim) / den
    ci = (ni * lre - nr * lim) / den
    return cr * bre - ci * bim, cr * bim + ci * bre


def _crows(pr, pi, xr, xi):
    nc = xr.shape[0]
    rr = jnp.concatenate([pr * xr[c:c + 1] - pi * xi[c:c + 1] for c in range(nc)], axis=0)
    ri = jnp.concatenate([pr * xi[c:c + 1] + pi * xr[c:c + 1] for c in range(nc)], axis=0)
    return rr, ri


def _chunk_rows(ut_ref, gi, uc3):
    bsz, nc2, kc, _ = ut_ref.shape
    nc = nc2 // 2
    for b in range(bsz):
        for c in range(nc):
            uc3[:, b, c * LANES:(c + 1) * LANES] = ut_ref[b, gi * nc + c]
    return uc3[...].reshape(kc * uc3.shape[1], nc * LANES).astype(BF16)


def _s5_local_kernel(ut_ref, lre_ref, lim_ref, ldt_ref, bre_ref, bim_ref, zr_ref, zi_ref, uc3, *, chunk):
    @pl.when(pl.program_id(0) == 0)
    def _():
        uc3[...] = jnp.zeros_like(uc3)

    lane = lax.broadcasted_iota(I32, (1, LANES), 1)
    e = (chunk - 1.0) - lax.broadcasted_iota(I32, (chunk, 1), 0).astype(F32)
    zr = None
    for gi in range(2):
        lre, lim = lre_ref[gi], lim_ref[gi]
        ldr, ldi = _ssm_consts(lre, lim, ldt_ref[gi])
        bbr, bbi = _bbar(lre, lim, ldr, ldi, bre_ref[gi], bim_ref[gi])
        pr, pi = _cpow(e, ldr, ldi)
        wr, wi = _crows(pr, pi, bbr, bbi)
        mine = (lane < LANES // 2) if gi == 0 else (lane >= LANES // 2)
        u = _chunk_rows(ut_ref, gi, uc3)
        pr_ = jnp.dot(u, jnp.where(mine, wr, 0.0).astype(BF16), preferred_element_type=F32)
        pi_ = jnp.dot(u, jnp.where(mine, wi, 0.0).astype(BF16), preferred_element_type=F32)
        zr, zi = (pr_, pi_) if zr is None else (zr + pr_, zi + pi_)
    zr_ref[...] = zr
    zi_ref[...] = zi


def _s5_scan_kernel(zr_ref, zi_ref, lre_ref, lim_ref, ldt_ref, xr_ref, xi_ref, sr, si, ar, ai, *, chunk, nblk):
    @pl.when(pl.program_id(0) == 0)
    def _():
        ldr, ldi = _ssm_consts(lre_ref[...], lim_ref[...], ldt_ref[...])
        mag = jnp.exp(chunk * ldr)
        ar[...] = jnp.broadcast_to(mag * jnp.cos(chunk * ldi), ar.shape)
        ai[...] = jnp.broadcast_to(mag * jnp.sin(chunk * ldi), ai.shape)
        sr[...] = jnp.zeros_like(sr)
        si[...] = jnp.zeros_like(si)

    a_r = ar[...]
    a_i = ai[...]

    def body(kk, carry):
        xr, xi = carry
        rows = pl.ds(pl.multiple_of(kk * SSM_BATCH_PAD, SSM_BATCH_PAD), SSM_BATCH_PAD)
        xr_ref[rows, :] = xr
        xi_ref[rows, :] = xi
        return (a_r * xr - a_i * xi + zr_ref[rows, :], a_r * xi + a_i * xr + zi_ref[rows, :])

    xr, xi = lax.fori_loop(0, nblk, body, (sr[...], si[...]))
    sr[...] = xr
    si[...] = xi


def _s5_out_kernel(ut_ref, xr_ref, xi_ref, lre_ref, lim_ref, ldt_ref, bre_ref, bim_ref, cre_ref, cim_ref,
                   yt_ref, uc3, y3, toep, kern, *, chunk):
    @pl.when(pl.program_id(0) == 0)
    def _():
        uc3[...] = jnp.zeros_like(uc3)

    bsz, nc2, kc, _ = ut_ref.shape
    nc = nc2 // 2
    lane = lax.broadcasted_iota(I32, (1, LANES), 1)
    low = lane < LANES // 2
    step = lax.broadcasted_iota(I32, (chunk, 1), 0).astype(F32)
    causal = lax.broadcasted_iota(I32, (chunk, chunk), 1) >= lax.broadcasted_iota(I32, (chunk, chunk), 0)
    xr = xr_ref[...].astype(BF16)
    xi = xi_ref[...].astype(BF16)
    for gi in range(2):
        lre, lim = lre_ref[gi], lim_ref[gi]
        ldr, ldi = _ssm_consts(lre, lim, ldt_ref[gi])
        bbr, bbi = _bbar(lre, lim, ldr, ldi, bre_ref[gi], bim_ref[gi])
        cre, cim = cre_ref[gi], cim_ref[gi]
        pr, pi = _cpow(step, ldr, ldi)
        gr, gim = _crows(pr, pi, cre, cim)
        kmat = _dot_t_split(jnp.where(low, bbr, -bbi), jnp.where(low, gr, gim))
        for c1 in range(nc):
            kern[c1] = kmat[c1:c1 + 1]

        def toeplitz_rows(c1, _):
            rows = pl.ds(pl.multiple_of(c1 * chunk, chunk), chunk)
            for c in range(nc):
                resp = jnp.broadcast_to(kern[c1, :, c * chunk:(c + 1) * chunk], (chunk, chunk))
                blk = pltpu.roll(resp, 0, 1, stride=1, stride_axis=0)
                toep[rows, c * chunk:(c + 1) * chunk] = jnp.where(causal, blk, 0.0).astype(BF16)
            return 0

        lax.fori_loop(0, nc, toeplitz_rows, 0)
        qr, qi = _cpow(step + 1.0, ldr, ldi)
        vr, vi = _crows(qr, qi, cre, cim)
        mine = low if gi == 0 else jnp.logical_not(low)
        y = jnp.dot(_chunk_rows(ut_ref, gi, uc3), toep[...], preferred_element_type=F32)
        y = y + _dot_t(xr, jnp.where(mine, vr, 0.0).astype(BF16))
        y = y - _dot_t(xi, jnp.where(mine, vi, 0.0).astype(BF16))
        y3[...] = y.reshape(y3.shape)
        for b in range(bsz):
            for c in range(nc):
                yt_ref[b, gi * nc + c] = y3[:, b, c * LANES:(c + 1) * LANES]


def _s5_core(ut4, lam_re, lam_im, b_re, b_im, c_re, c_im, log_dt):
    bsz, _, kc, chunk = ut4.shape
    g, n = lam_re.shape
    nc = SSM_GROUP
    bp = SSM_BATCH_PAD
    rows = kc * bp
    r = nc * chunk
    assert 2 * n == LANES and chunk == LANES and g % 2 == 0 and bsz <= bp and kc % min(SCAN_BLOCK, kc) == 0

    dup = lambda a: jnp.concatenate([a, a], axis=-1).astype(F32)
    lre2 = dup(lam_re)[:, None, :]
    lim2 = dup(lam_im)[:, None, :]
    ldt2 = jnp.broadcast_to(log_dt.astype(F32)[:, None, None], (g, 1, LANES))
    bre2 = dup(b_re.transpose(0, 2, 1))
    bim2 = dup(b_im.transpose(0, 2, 1))
    cre2 = dup(c_re)
    cim2 = dup(c_im)

    pair3 = lambda p: (p, 0, 0)
    vec_spec = pl.BlockSpec((2, 1, LANES), pair3)
    coef_spec = pl.BlockSpec((2, nc, LANES), pair3)
    u_spec = pl.BlockSpec((bsz, 2 * nc, kc, chunk), lambda p: (0, p, 0, 0))
    z_spec = pl.BlockSpec((rows, LANES), lambda p: (0, p))
    params = pltpu.CompilerParams(dimension_semantics=("arbitrary",), vmem_limit_bytes=VMEM_LIMIT)
    rows3 = pltpu.VMEM((kc, bp, r), F32)

    zr, zi = pl.pallas_call(
        functools.partial(_s5_local_kernel, chunk=chunk),
        out_shape=(jax.ShapeDtypeStruct((rows, g * n), F32),) * 2,
        grid=(g // 2,),
        in_specs=[u_spec, vec_spec, vec_spec, vec_spec, coef_spec, coef_spec],
        out_specs=(z_spec, z_spec),
        scratch_shapes=[rows3],
        compiler_params=params, name="s5_local",
    )(ut4, lre2, lim2, ldt2, bre2, bim2)

    nblk = min(SCAN_BLOCK, kc)
    flat = lambda a: a.astype(F32).reshape(1, g * n)
    ldt_flat = jnp.broadcast_to(log_dt.astype(F32)[:, None], (g, n)).reshape(1, g * n)
    blk = pl.BlockSpec((nblk * bp, g * n), lambda i: (i, 0))
    cst = pl.BlockSpec((1, g * n), lambda i: (0, 0))
    xr, xi = pl.pallas_call(
        functools.partial(_s5_scan_kernel, chunk=chunk, nblk=nblk),
        out_shape=(jax.ShapeDtypeStruct((rows, g * n), F32),) * 2,
        grid=(kc // nblk,),
        in_specs=[blk, blk, cst, cst, cst],
        out_specs=(blk, blk),
        scratch_shapes=[pltpu.VMEM((bp, g * n), F32)] * 4,
        compiler_params=params, name="s5_scan",
    )(zr, zi, flat(lam_re), flat(lam_im), ldt_flat)

    return pl.pallas_call(
        functools.partial(_s5_out_kernel, chunk=chunk),
        out_shape=jax.ShapeDtypeStruct(ut4.shape, F32),
        grid=(g // 2,),
        in_specs=[u_spec, z_spec, z_spec, vec_spec, vec_spec, vec_spec, coef_spec, coef_spec, coef_spec, coef_spec],
        out_specs=u_spec,
        scratch_shapes=[rows3, rows3, pltpu.VMEM((r, r), BF16), pltpu.VMEM((nc, 1, r), F32)],
        compiler_params=params, name="s5_out",
    )(ut4, xr, xi, lre2, lim2, ldt2, bre2, bim2, cre2, cim2)


def _mix_kernel(x_ref, a_ref, y_ref, u_ref, d_ref, wglu_ref, bglu_ref, gs_ref, wout_ref, gffn_ref,
                wr_ref, wrlo_ref, br_ref, h_ref, hn_ref, idx_ref, gate_ref, rank_ref, cnt_ref, run_ref, *, d_attn):
    i = pl.program_id(0)

    @pl.when(i == 0)
    def _():
        run_ref[...] = jnp.zeros_like(run_ref)

    tm = min(MIX_SUBTILE, x_ref.shape[0])
    nch = x_ref.shape[1] // LANES
    for sub in range(x_ref.shape[0] // tm):
        _mix_subtile(sub, tm, nch, d_attn, x_ref, a_ref, y_ref, u_ref, d_ref, wglu_ref, bglu_ref, gs_ref,
                     wout_ref, gffn_ref, wr_ref, wrlo_ref, br_ref, h_ref, hn_ref, idx_ref, gate_ref, rank_ref, run_ref)
    cnt_ref[...] = run_ref[...]


def _mix_subtile(sub, tm, nch, d_attn, x_ref, a_ref, y_ref, u_ref, d_ref, wglu_ref, bglu_ref, gs_ref, wout_ref,
                 gffn_ref, wr_ref, wrlo_ref, br_ref, h_ref, hn_ref, idx_ref, gate_ref, rank_ref, run_ref):
    rows = pl.ds(sub * tm, tm)
    cps = tm // LANES
    yt = jnp.concatenate([y_ref[:, sub * cps + c, :] for c in range(cps)], axis=-1)
    y = yt.T + d_ref[...] * u_ref[rows, :].astype(F32)
    y = 0.5 * y * (1.0 + lax.erf(y * (0.5 ** 0.5)))
    z = jnp.dot(y.astype(BF16), wglu_ref[...], preferred_element_type=F32) + bglu_ref[...]
    y = y * jax.nn.sigmoid(z)
    s_out = _rms(y, gs_ref[...])
    mix = (jnp.dot(a_ref[rows, :], wout_ref[:d_attn, :], preferred_element_type=F32)
           + jnp.dot(s_out.astype(BF16), wout_ref[d_attn:, :], preferred_element_type=F32))
    h = x_ref[rows, :] + mix
    h_ref[rows, :] = h
    hn = _rms(h, gffn_ref[...])
    for c in range(nch):
        hn_ref[pl.ds(sub * tm * nch + c, tm, stride=nch), :] = hn[:, c * LANES:(c + 1) * LANES]

    hn_hi = hn.astype(BF16)
    hn_lo = (hn - hn_hi.astype(F32)).astype(BF16)
    lg = (jnp.dot(hn_hi, wr_ref[...], preferred_element_type=F32)
          + jnp.dot(hn_hi, wrlo_ref[...], preferred_element_type=F32)
          + jnp.dot(hn_lo, wr_ref[...], preferred_element_type=F32)) + br_ref[...]
    lane = lax.broadcasted_iota(I32, (tm, LANES), 1)
    lane_f = lane.astype(F32)
    vals, hots = [], []
    idx_all = jnp.zeros((tm, LANES), F32)
    for kk in range(TOP_K):
        mx = jnp.max(lg, axis=-1, keepdims=True)
        ix = jnp.min(jnp.where(lg == mx, lane_f, float(LANES)), axis=-1, keepdims=True)
        hot = lane_f == ix
        vals.append(mx)
        hots.append(hot)
        idx_all = jnp.where(lane == kk, ix, idx_all)
        lg = jnp.where(hot, -3e38, lg)
    ex = [jnp.exp(v - vals[0]) for v in vals]
    den = ex[0] + ex[1] + ex[2] + ex[3]
    gate_all = jnp.zeros((tm, LANES), F32)
    for kk in range(TOP_K):
        gate_all = jnp.where(lane == kk, ex[kk] / den, gate_all)

    picked = jnp.zeros((tm, LANES), F32)
    for hot in hots:
        picked = jnp.where(hot, 1.0, picked)
    tri = (lax.broadcasted_iota(I32, (tm, tm), 1) < lax.broadcasted_iota(I32, (tm, tm), 0))
    before = jnp.dot(jnp.where(tri, 1.0, 0.0).astype(BF16), picked.astype(BF16), preferred_element_type=F32)
    before = before + run_ref[...]
    rank_all = jnp.zeros((tm, LANES), F32)
    for kk, hot in enumerate(hots):
        rk = jnp.sum(jnp.where(hot, before, 0.0), axis=-1, keepdims=True)
        rank_all = jnp.where(lane == kk, rk, rank_all)
    run_ref[...] = run_ref[...] + jnp.sum(picked, axis=0, keepdims=True)
    idx_ref[rows, :] = idx_all[:, :TOP_K].astype(I32)
    gate_ref[rows, :] = gate_all[:, :TOP_K]
    rank_ref[rows, :] = rank_all[:, :TOP_K].astype(I32)


def _mix(x2, a_out, yt4, u, ssm_d, w_glu, b_glu, ssm_norm_g, w_out, ln_ffn_g, w_router, b_router):
    t, d = x2.shape
    d_attn = a_out.shape[1]
    bsz, d_ssm, kc, _ = yt4.shape
    s_len = t // bsz
    n_exp = w_router.shape[1]
    tm = min(TOKEN_TILE, s_len)
    tiles = s_len // tm
    nch = d // LANES
    wr = jnp.zeros((d, LANES), F32).at[:, :n_exp].set(w_router.astype(F32))
    br = jnp.full((1, LANES), NEG_BIG, F32).at[0, :n_exp].set(b_router.astype(F32))
    wr_hi = wr.astype(BF16)
    wr_lo = (wr - wr_hi.astype(F32)).astype(BF16)
    row = lambda i: (i, 0)
    const = lambda i: (0, 0)
    vec = lambda a: a.reshape(1, -1).astype(F32)
    return pl.pallas_call(
        functools.partial(_mix_kernel, d_attn=d_attn),
        out_shape=(jax.ShapeDtypeStruct((t, d), F32), jax.ShapeDtypeStruct((t * nch, LANES), F32),
                   jax.ShapeDtypeStruct((t, TOP_K), I32), jax.ShapeDtypeStruct((t, TOP_K), F32),
                   jax.ShapeDtypeStruct((t, TOP_K), I32), jax.ShapeDtypeStruct((1, LANES), F32)),
        grid=(t // tm,),
        in_specs=[pl.BlockSpec((tm, d), row), pl.BlockSpec((tm, d_attn), row),
                  pl.BlockSpec((None, d_ssm, tm // LANES, LANES), lambda i: (i // tiles, 0, i % tiles, 0)),
                  pl.BlockSpec((tm, d_ssm), row), pl.BlockSpec((1, d_ssm), const),
                  pl.BlockSpec((d_ssm, d_ssm), const), pl.BlockSpec((1, d_ssm), const),
                  pl.BlockSpec((1, d_ssm), const), pl.BlockSpec((d_attn + d_ssm, d), const),
                  pl.BlockSpec((1, d), const), pl.BlockSpec((d, LANES), const), pl.BlockSpec((d, LANES), const),
                  pl.BlockSpec((1, LANES), const)],
        out_specs=(pl.BlockSpec((tm, d), row), pl.BlockSpec((tm * nch, LANES), row),
                   pl.BlockSpec((tm, TOP_K), row), pl.BlockSpec((tm, TOP_K), row), pl.BlockSpec((tm, TOP_K), row),
                   pl.BlockSpec((1, LANES), const)),
        scratch_shapes=[pltpu.VMEM((1, LANES), F32)],
        compiler_params=pltpu.CompilerParams(dimension_semantics=("arbitrary",), vmem_limit_bytes=VMEM_LIMIT),
        name="mix",
    )(x2, a_out, yt4, u, vec(ssm_d), w_glu.astype(BF16), vec(b_glu), vec(ssm_norm_g), w_out.astype(BF16),
      vec(ln_ffn_g), wr_hi, wr_lo, br)


IDX_RING = 4


def _moe_kernel(blk_e_ref, nused_ref, slot_hbm, hn_hbm, wgu_ref, bgu_ref, wd_ref, bd_ref, ytok_hbm,
                idx_smem, xbuf, ybuf, wperm, wdown, idx_sem, g_sem, s_sem, *,
                n_tok, n_blocks, tok_pad, rows, nch):
    i = pl.program_id(0)
    n_used = nused_ref[0]
    d_ff = wdown.shape[0]
    grp = 2 * LANES

    def idx_copy(blk, src=None):
        src = jnp.minimum(blk, n_blocks - 1) if src is None else src
        slot = blk & (IDX_RING - 1)
        return pltpu.make_async_copy(slot_hbm.at[src], idx_smem.at[slot], idx_sem.at[slot])

    blk_rows = rows * nch

    def half(par):
        return pl.ds(par * blk_rows, blk_rows)

    def gather_wait(par):
        pltpu.make_async_copy(hn_hbm.at[pl.ds(0, blk_rows)], xbuf.at[half(par)], g_sem.at[par]).wait()

    def scatter_wait(par):
        pltpu.make_async_copy(ybuf.at[half(par)], ytok_hbm.at[pl.ds(0, blk_rows)], s_sem.at[par]).wait()

    def issue_gather(blk, par):
        slot = blk & (IDX_RING - 1)
        for r in range(rows):
            src = pl.ds(pl.multiple_of(idx_smem[slot, 0, r], nch), nch)
            pltpu.make_async_copy(hn_hbm.at[src], xbuf.at[pl.ds(par * blk_rows + r * nch, nch)],
                                  g_sem.at[par]).start(priority=r % 2)

    def issue_scatter(blk, par):
        slot = blk & (IDX_RING - 1)
        for r in range(rows):
            dst = pl.ds(pl.multiple_of(idx_smem[slot, 1, r], nch), nch)
            pltpu.make_async_copy(ybuf.at[pl.ds(par * blk_rows + r * nch, nch)], ytok_hbm.at[dst],
                                  s_sem.at[par]).start(priority=r % 2)

    @pl.when(i == 0)
    def _():
        idx_copy(0).start()
        idx_copy(1).start()
        idx_copy(-1, src=n_blocks - 1).start()
        spare = (tok_pad - n_tok) * nch
        ybuf[...] = jnp.zeros_like(ybuf)
        fills = [pltpu.make_async_copy(ybuf.at[pl.ds(0, spare)],
                                       ytok_hbm.at[pl.ds((kk * tok_pad + n_tok) * nch, spare)], s_sem.at[0])
                 for kk in range(TOP_K)]
        for f in fills:
            f.start()
        for f in fills:
            f.wait()
        idx_copy(0).wait()
        idx_copy(-1, src=n_blocks - 1).wait()
        issue_gather(0, 0)

    def block(par):
        idx_copy(i + 1).wait()
        gather_wait(par)

        @pl.when(i > 0)
        def _():
            scatter_wait(par)

        expert = blk_e_ref[i]

        @pl.when(jnp.logical_or(i == 0, expert != blk_e_ref[jnp.maximum(i - 1, 0)]))
        def _():
            src = lax.broadcasted_iota(I32, (grp, grp), 0)
            col = lax.broadcasted_iota(I32, (grp, grp), 1)
            want = jnp.where(col < LANES, 2 * col, 2 * (col - LANES) + 1)
            perm = jnp.where(src == want, 1.0, 0.0).astype(BF16)
            for gi in range(wperm.shape[1] // grp):
                cols = slice(gi * grp, (gi + 1) * grp)
                wperm[:, cols] = jnp.dot(wgu_ref[:, cols].astype(BF16), perm,
                                         preferred_element_type=F32).astype(BF16)
            wdown[...] = wd_ref[...].astype(BF16)

        issue_gather(i + 1, 1 - par)
        issue_scatter(i - 1, 1 - par)
        x = jnp.concatenate([xbuf[pl.ds(par * blk_rows + c, rows, stride=nch), :] for c in range(nch)],
                            axis=-1).astype(BF16)
        gu = jnp.dot(x, wperm[...], preferred_element_type=F32) + bgu_ref[...]
        acts = []
        for gi in range(d_ff // LANES):
            gate = jnp.minimum(gu[:, gi * grp:gi * grp + LANES], SWIGLU_LIMIT)
            up = jnp.clip(gu[:, gi * grp + LANES:(gi + 1) * grp], -SWIGLU_LIMIT, SWIGLU_LIMIT)
            acts.append((gate * jax.nn.sigmoid(SWIGLU_ALPHA * gate) * (up + 1.0)).astype(BF16))
        y = jnp.dot(jnp.concatenate(acts, axis=-1), wdown[...], preferred_element_type=F32) + bd_ref[...]
        for c in range(nch):
            ybuf[pl.ds(par * blk_rows + c, rows, stride=nch), :] = y[:, c * LANES:(c + 1) * LANES]
        idx_copy(i + 2).start()

        @pl.when(i == n_used - 1)
        def _():
            issue_scatter(i, par)
            scatter_wait(1 - par)
            scatter_wait(par)
            gather_wait(1 - par)
            idx_copy(i + 2).wait()

    for par in range(2):
        pl.when(jnp.logical_and(i < n_used, (i & 1) == par))(functools.partial(block, par))


def _moe(hn2, slot_rows, blk_expert, n_used, wgu, bgu, wd, bd, n_tok, tok_pad):
    n_blocks, _, rows = slot_rows.shape
    n_exp, d, d_ff2 = wgu.shape
    d_ff = d_ff2 // 2
    nch = d // LANES
    wmap = lambda i, be, nu: (be[i], 0, 0)
    return pl.pallas_call(
        functools.partial(_moe_kernel, n_tok=n_tok, n_blocks=n_blocks, tok_pad=tok_pad, rows=rows, nch=nch),
        out_shape=jax.ShapeDtypeStruct((TOP_K * tok_pad * nch, LANES), F32),
        grid_spec=pltpu.PrefetchScalarGridSpec(
            num_scalar_prefetch=2,
            grid=(n_blocks,),
            in_specs=[pl.BlockSpec(memory_space=pl.ANY), pl.BlockSpec(memory_space=pl.ANY),
                      pl.BlockSpec((None, d, d_ff2), wmap), pl.BlockSpec((None, 1, d_ff2), wmap),
                      pl.BlockSpec((None, d_ff, d), wmap), pl.BlockSpec((None, 1, d), wmap)],
            out_specs=pl.BlockSpec(memory_space=pl.ANY),
            scratch_shapes=[pltpu.SMEM((IDX_RING, 2, rows), I32), pltpu.VMEM((2 * rows * nch, LANES), F32),
                            pltpu.VMEM((2 * rows * nch, LANES), F32), pltpu.VMEM((d, d_ff2), BF16),
                            pltpu.VMEM((d_ff, d), BF16), pltpu.SemaphoreType.DMA((IDX_RING,)),
                            pltpu.SemaphoreType.DMA((2,)), pltpu.SemaphoreType.DMA((2,))]),
        compiler_params=pltpu.CompilerParams(dimension_semantics=("arbitrary",), vmem_limit_bytes=VMEM_LIMIT),
        name="moe",
    )(blk_expert, n_used, slot_rows, hn2, wgu, bgu, wd, bd)


def _combine_kernel(h_ref, y_ref, gate_ref, g_ref, o_ref):
    tm, d = h_ref.shape
    nch = d // LANES
    gates = gate_ref[...]
    parts = []
    ss = None
    for c in range(nch):
        acc = h_ref[:, c * LANES:(c + 1) * LANES]
        for kk in range(TOP_K):
            acc = acc + gates[:, kk:kk + 1] * y_ref[kk, pl.ds(c, tm, stride=nch), :]
        parts.append(acc)
        sq = jnp.sum(acc * acc, axis=-1, keepdims=True)
        ss = sq if ss is None else ss + sq
    inv = lax.rsqrt(ss / (nch * LANES) + RMS_EPS)
    for c in range(nch):
        o_ref[:, c * LANES:(c + 1) * LANES] = parts[c] * inv * g_ref[:, c * LANES:(c + 1) * LANES]


def _combine(h, ytok3, gates, final_g):
    t, d = h.shape
    nch = d // LANES
    tm = min(TOKEN_TILE // 2, t)
    return pl.pallas_call(
        _combine_kernel,
        out_shape=jax.ShapeDtypeStruct((t, d), F32),
        grid=(t // tm,),
        in_specs=[pl.BlockSpec((tm, d), lambda i: (i, 0)),
                  pl.BlockSpec((TOP_K, tm * nch, LANES), lambda i: (0, i, 0)),
                  pl.BlockSpec((tm, TOP_K), lambda i: (i, 0)), pl.BlockSpec((1, d), lambda i: (0, 0))],
        out_specs=pl.BlockSpec((tm, d), lambda i: (i, 0)),
        compiler_params=pltpu.CompilerParams(dimension_semantics=("arbitrary",), vmem_limit_bytes=VMEM_LIMIT),
        name="combine",
    )(h, ytok3, gates, final_g.reshape(1, d).astype(F32))


def _moe_ffn(h, hn2, idx, gates, rank, counts, w_gate_up, b_gate_up, w_down, b_down, final_g):
    t, d = h.shape
    n_exp = w_gate_up.shape[0]
    d_ff = w_down.shape[1]
    rows = MOE_ROWS
    n_assign = t * TOP_K
    n_rows = n_assign + n_exp * rows
    n_blocks = n_rows // rows
    tok_pad = t + (2 * rows) // TOP_K

    cnt = counts[0, :n_exp].astype(I32)
    padded = (cnt + rows - 1) // rows * rows
    pad_ends = jnp.cumsum(padded)
    pad_starts = pad_ends - padded
    dest = pad_starts[idx] + rank
    dump = n_assign + jnp.arange(n_rows, dtype=I32) % (2 * rows)
    slot_rows = dump.at[dest.reshape(-1)].set(jnp.arange(n_assign, dtype=I32), unique_indices=True)
    blk_start = jnp.arange(n_blocks, dtype=I32) * rows
    blk_expert = jnp.minimum(jnp.sum((pad_ends[None, :] <= blk_start[:, None]).astype(I32), axis=1), n_exp - 1)
    n_used = (pad_ends[-1:] // rows).astype(I32)

    bgu = (b_gate_up.astype(F32).reshape(n_exp, d_ff // LANES, LANES, 2).transpose(0, 1, 3, 2)
           .reshape(n_exp, 1, 2 * d_ff))
    nch = d // LANES
    src_row = jnp.minimum(slot_rows >> 2, t - 1) * nch
    dst_row = ((slot_rows & (TOP_K - 1)) * tok_pad + (slot_rows >> 2)) * nch
    row_idx = jnp.stack([src_row.reshape(n_blocks, rows), dst_row.reshape(n_blocks, rows)], axis=1)
    ytok = _moe(hn2, row_idx, blk_expert, n_used, w_gate_up, bgu,
                w_down, b_down.reshape(n_exp, 1, d).astype(F32), t, tok_pad)
    return _combine(h, ytok.reshape(TOP_K, tok_pad * (d // LANES), LANES), gates, final_g)


def kernel(x, positions, ln_mix_g, w_in, lam_q1, lam_k1, lam_q2, lam_k2, diff_norm_g, ssm_lam_re, ssm_lam_im,
           ssm_b_re, ssm_b_im, ssm_c_re, ssm_c_im, ssm_d, ssm_log_dt, ssm_w_glu, ssm_b_glu, ssm_norm_g, w_out,
           ln_ffn_g, w_router, b_router, w_gate_up, b_gate_up, w_down, b_down, final_norm_g):
    bsz, s_len, d = x.shape
    depth = w_in.shape[0]
    assert depth == 1
    li = 0
    lambda_init = 0.8 - 0.6 * math.exp(-0.3 * li)
    d_attn = ATTN_HEADS * LANES
    d_ssm = w_in.shape[2] - 3 * d_attn
    t = bsz * s_len

    x2 = x.reshape(t, d)
    q, k, v, u, ut4 = _inproj(x2, positions.reshape(t, 1), ln_mix_g[li], w_in[li], d_attn, d_ssm, bsz)
    vt = v.reshape(bsz, s_len, d_attn).transpose(0, 2, 1)
    a_out = _attention(q.reshape(bsz, s_len, d_attn), k.reshape(bsz, s_len, d_attn), vt,
                       lam_q1[li], lam_k1[li], lam_q2[li], lam_k2[li], diff_norm_g[li], lambda_init)
    yt4 = _s5_core(ut4, ssm_lam_re[li], ssm_lam_im[li], ssm_b_re[li], ssm_b_im[li],
                   ssm_c_re[li], ssm_c_im[li], ssm_log_dt[li])
    h, hn2, idx, gates, rank, counts = _mix(
        x2, a_out.reshape(t, d_attn), yt4, u, ssm_d[li], ssm_w_glu[li], ssm_b_glu[li], ssm_norm_g[li],
        w_out[li], ln_ffn_g[li], w_router[li], b_router[li])
    out = _moe_ffn(h, hn2, idx, gates, rank, counts, w_gate_up[li], b_gate_up[li], w_down[li], b_down[li],
                   final_norm_g)
    return out.reshape(bsz, s_len, d)
```
